```python
import jax
import jax.numpy as jnp
from jax import lax
import numpy as np

D_MODEL = 2048
BATCH = 4
SEQ = 4096
DEPTH = 4
DEC_BATCH = 16
DEC_SEQ = 64
PAST_LEN = 4096

CHUNK = 64
N_MIXERS = 3
N_LAYERS_A = (DEPTH + 2) // 3
N_LAYERS_B = (DEPTH + 1) // 3
N_LAYERS_C = DEPTH // 3
EPS = 1e-6
A_CONV = 3
DN_QK_HEADS = 16
DN_V_HEADS = 32
DN_HEAD_DIM = 128
DN_KEY_DIM = DN_QK_HEADS * DN_HEAD_DIM
DN_VAL_DIM = DN_V_HEADS * DN_HEAD_DIM
DN_CONV = 4
DN_CONV_DIM = 2 * DN_KEY_DIM + DN_VAL_DIM
DN_IN_DIM = DN_CONV_DIM + DN_VAL_DIM + 2 * DN_V_HEADS
DN_CHUNK = 64
ATT_HEADS = 16
ATT_HEAD_DIM = 128
LEFT_CHUNKS = 8
WINDOW = LEFT_CHUNKS * CHUNK
BAND = WINDOW + CHUNK
REL_CLIP = 128
D_FF = 5632
FFN_CONV = 3
PLE_DIM = 256
NEG_INF = -1e30

kernel_name = 'hybrid_streaming_encoder_step'


def rms_norm(x, g):
    xf = x.astype(jnp.float32)
    y = xf * lax.rsqrt(jnp.mean(xf * xf, axis=-1, keepdims=True) + EPS)
    return (y * g.astype(jnp.float32)).astype(x.dtype)


def l2_norm(x):
    xf = x.astype(jnp.float32)
    return xf * lax.rsqrt(jnp.sum(xf * xf, axis=-1, keepdims=True) + EPS)


def causal_dwconv(x, hist, w, b=None):
    width = w.shape[0]
    s = x.shape[1]
    xx = jnp.concatenate([hist.astype(x.dtype), x], axis=1)
    y = sum(xx[:, t:t + s] * w[t] for t in range(width))
    if b is not None:
        y = y + b
    return y, xx[:, s:]


def short_conv_mixer(h, conv_hist, w_in, w_conv, w_out):
    b_gate, c_gate, v = jnp.split(h @ w_in, 3, axis=-1)
    y, new_hist = causal_dwconv(c_gate * v, conv_hist, w_conv)
    return (b_gate * y) @ w_out, new_hist


def chunk_gated_delta_rule(q, k, v, g, beta, state0):
    bsz, s, nh, _ = k.shape
    dv = v.shape[-1]
    pad = (-s) % DN_CHUNK
    nc = (s + pad) // DN_CHUNK

    def to_chunks(t):
        t = jnp.pad(t, [(0, 0), (0, pad)] + [(0, 0)] * (t.ndim - 2))
        t = t.reshape((bsz, nc, DN_CHUNK) + t.shape[2:])
        return jnp.swapaxes(jnp.swapaxes(t, 2, 3), 0, 1)

    qc, kc, vc, gc, bc = (to_chunks(t) for t in (q, k, v, g, beta))
    gcum = jnp.cumsum(gc, axis=-1)
    tril = jnp.tril(jnp.ones((DN_CHUNK, DN_CHUNK), dtype=bool))
    strict = jnp.tril(jnp.ones((DN_CHUNK, DN_CHUNK), dtype=bool), -1)
    diff = gcum[..., :, None] - gcum[..., None, :]
    decay = jnp.where(tril, jnp.exp(jnp.where(tril, diff, 0.0)), 0.0)
    kb = kc * bc[..., None]
    lmat = jnp.where(strict, jnp.einsum('nbhid,nbhjd->nbhij', kb, kc) * decay, 0.0)
    eye = jnp.eye(DN_CHUNK, dtype=lmat.dtype)
    tmat = lax.linalg.triangular_solve(eye + lmat, jnp.broadcast_to(eye, lmat.shape),
                                       left_side=True, lower=True)
    u = jnp.einsum('nbhij,nbhje->nbhie', tmat, vc * bc[..., None])
    w = jnp.einsum('nbhij,nbhjd->nbhid', tmat, kb * jnp.exp(gcum)[..., None])
    a_intra = jnp.where(tril, jnp.einsum('nbhid,nbhjd->nbhij', qc, kc) * decay, 0.0)

    def step(state, inp):
        q_i, k_i, u_i, w_i, g_i, a_i = inp
        v_new = u_i - jnp.einsum('bhcd,bhde->bhce', w_i, state)
        o_i = (jnp.einsum('bhcd,bhde->bhce', q_i * jnp.exp(g_i)[..., None], state)
               + jnp.einsum('bhij,bhje->bhie', a_i, v_new))
        g_last = g_i[..., -1:]
        state = (state * jnp.exp(g_last)[..., None]
                 + jnp.einsum('bhcd,bhce->bhde', k_i * jnp.exp(g_last - g_i)[..., None], v_new))
        return state, o_i

    state, o = lax.scan(step, state0, (qc, kc, u, w, gcum, a_intra))
    o = jnp.transpose(o, (1, 0, 3, 2, 4)).reshape(bsz, nc * DN_CHUNK, nh, dv)[:, :s]
    return o, state


def gated_delta_mixer(h, conv_hist, ssm_state, w_in, w_conv, a_log, dt_bias, g_out, w_out):
    bsz, s, _ = h.shape
    proj = h @ w_in
    qkv, z, b, a = jnp.split(proj, [DN_CONV_DIM, DN_CONV_DIM + DN_VAL_DIM,
                                    DN_CONV_DIM + DN_VAL_DIM + DN_V_HEADS], axis=-1)
    qkv, new_conv = causal_dwconv(qkv, conv_hist, w_conv)
    qkv = jax.nn.silu(qkv)
    q, k, v = jnp.split(qkv, [DN_KEY_DIM, 2 * DN_KEY_DIM], axis=-1)
    rep = DN_V_HEADS // DN_QK_HEADS
    q = jnp.repeat(l2_norm(q.reshape(bsz, s, DN_QK_HEADS, DN_HEAD_DIM)), rep, axis=2) * (DN_HEAD_DIM ** -0.5)
    k = jnp.repeat(l2_norm(k.reshape(bsz, s, DN_QK_HEADS, DN_HEAD_DIM)), rep, axis=2)
    v = v.reshape(bsz, s, DN_V_HEADS, DN_HEAD_DIM).astype(jnp.float32)
    beta = jax.nn.sigmoid(b.astype(jnp.float32))
    g = -jnp.exp(a_log.astype(jnp.float32)) * jax.nn.softplus(a.astype(jnp.float32) + dt_bias.astype(jnp.float32))
    o, new_state = chunk_gated_delta_rule(q, k, v, g, beta, ssm_state.astype(jnp.float32))
    o = rms_norm(o, g_out) * jax.nn.silu(z.reshape(bsz, s, DN_V_HEADS, DN_HEAD_DIM).astype(jnp.float32))
    out = o.reshape(bsz, s, DN_VAL_DIM).astype(h.dtype) @ w_out
    return out, new_conv, new_state.astype(ssm_state.dtype)


def chunk_band_attention(h, k_hist, v_hist, w_qkv, g_q, g_k, rel_bias, w_out):
    bsz, s, _ = h.shape
    qkv = (h @ w_qkv).reshape(bsz, s, 3, ATT_HEADS, ATT_HEAD_DIM)
    q = rms_norm(qkv[:, :, 0], g_q)
    k = rms_norm(qkv[:, :, 1], g_k)
    v = qkv[:, :, 2]
    n_hist = k_hist.shape[1]
    pad_r = (-s) % CHUNK
    nc = (s + pad_r) // CHUNK

    def extend(new, hist):
        return jnp.concatenate([
            jnp.zeros((bsz, WINDOW - n_hist, ATT_HEADS, ATT_HEAD_DIM), new.dtype),
            hist.astype(new.dtype), new,
            jnp.zeros((bsz, pad_r, ATT_HEADS, ATT_HEAD_DIM), new.dtype)], axis=1)

    k_ext = extend(k, k_hist)
    v_ext = extend(v, v_hist)
    q_pad = jnp.pad(q, ((0, 0), (0, pad_r), (0, 0), (0, 0)))
    rows = jnp.arange(WINDOW + nc * CHUNK)
    valid = (rows >= WINDOW - n_hist) & (rows < WINDOW + s)
    rel = WINDOW + jnp.arange(CHUNK)[:, None] - jnp.arange(BAND)[None, :]
    bias = rel_bias.astype(jnp.float32)[:, jnp.clip(rel, -REL_CLIP, REL_CLIP) + REL_CLIP]
    scale = ATT_HEAD_DIM ** -0.5

    def one_chunk(c):
        start = c * CHUNK
        q_c = lax.dynamic_slice_in_dim(q_pad, start, CHUNK, axis=1)
        k_c = lax.dynamic_slice_in_dim(k_ext, start, BAND, axis=1)
        v_c = lax.dynamic_slice_in_dim(v_ext, start, BAND, axis=1)
        m_c = lax.dynamic_slice_in_dim(valid, start, BAND)
        sc = jnp.einsum('bqhd,bkhd->bhqk', q_c, k_c).astype(jnp.float32) * scale + bias
        p = jax.nn.softmax(jnp.where(m_c, sc, NEG_INF), axis=-1)
        return jnp.einsum('bhqk,bkhd->bqhd', p.astype(v_c.dtype), v_c)

    o = lax.map(one_chunk, jnp.arange(nc))
    o = jnp.moveaxis(o, 0, 1).reshape(bsz, nc * CHUNK, ATT_HEADS * ATT_HEAD_DIM)[:, :s]
    return o @ w_out, k, v


def conv_glu_ffn(h, conv_hist, w_up, w_conv, b_conv, w_down):
    u, new_hist = causal_dwconv(h @ w_up, conv_hist, w_conv, b_conv)
    gate, up = jnp.split(u, 2, axis=-1)
    return (jax.nn.silu(gate) * up) @ w_down, new_hist


def per_layer_embed(h, p, g_norm, w_gate, w_proj):
    gate = jax.nn.sigmoid(rms_norm(h, g_norm) @ w_gate)
    return h + (p @ w_proj) * gate


def run_trunk(x, p, conv_a, conv_b, ssm_b, hist_k, hist_v, conv_f, prm):
    h = x
    s = x.shape[1]
    keep = min(WINDOW, s)
    out_a, out_bc, out_bs, out_k, out_v, out_f = [], [], [], [], [], []
    for i in range(DEPTH):
        kind, j = i % N_MIXERS, i // N_MIXERS
        hn = rms_norm(h, prm['g_mix'][i])
        if kind == 0:
            y, st = short_conv_mixer(hn, conv_a[j], prm['w_a_in'][j], prm['w_a_conv'][j], prm['w_a_out'][j])
            out_a.append(st)
        elif kind == 1:
            y, st_c, st_s = gated_delta_mixer(hn, conv_b[j], ssm_b[j], prm['w_b_in'][j], prm['w_b_conv'][j],
                                              prm['b_a_log'][j], prm['b_dt_bias'][j], prm['g_b_out'][j],
                                              prm['w_b_out'][j])
            out_bc.append(st_c)
            out_bs.append(st_s)
        else:
            y, k_new, v_new = chunk_band_attention(hn, hist_k[j], hist_v[j], prm['w_c_qkv'][j], prm['g_c_q'][j],
                                                   prm['g_c_k'][j], prm['c_rel_bias'][j], prm['w_c_out'][j])
            out_k.append(k_new[:, s - keep:])
            out_v.append(v_new[:, s - keep:])
        h = h + y
        y, st = conv_glu_ffn(rms_norm(h, prm['g_ffn'][i]), conv_f[i], prm['w_f_up'][i], prm['w_f_conv'][i],
                             prm['b_f_conv'][i], prm['w_f_down'][i])
        out_f.append(st)
        h = h + y
        h = per_layer_embed(h, p[i], prm['g_ple'][i], prm['w_ple_gate'][i], prm['w_ple_proj'][i])
    return (rms_norm(h, prm['g_final']), jnp.stack(out_a), jnp.stack(out_bc), jnp.stack(out_bs),
            jnp.stack(out_k), jnp.stack(out_v), jnp.stack(out_f))


def setup_inputs(seed: int = 0) -> dict:
    key = jax.random.key(seed)
    ks = iter(jax.random.split(key, 48))

    def nrm(shape, scale):
        return jax.random.normal(next(ks), shape, jnp.float32) * scale

    def gain(shape):
        return 1.0 + nrm(shape, 0.05)

    c_len = min(WINDOW, PAST_LEN)
    out_scale = (2 * DEPTH) ** -0.5
    a_log = jnp.log(jax.random.uniform(next(ks), (N_LAYERS_B, DN_V_HEADS), jnp.float32, 1.0, 16.0))
    dt = jnp.exp(jax.random.uniform(next(ks), (N_LAYERS_B, DN_V_HEADS), jnp.float32,
                                    float(np.log(1e-3)), float(np.log(1e-1))))
    dt_bias = dt + jnp.log(-jnp.expm1(-dt))
    return {
        'x_prompt': nrm((BATCH, SEQ, D_MODEL), 1.0),
        'x_sample': nrm((DEC_BATCH, DEC_SEQ, D_MODEL), 1.0),
        'state_a_conv': nrm((N_LAYERS_A, DEC_BATCH, A_CONV - 1, D_MODEL), 1.0),
        'state_b_conv': nrm((N_LAYERS_B, DEC_BATCH, DN_CONV - 1, DN_CONV_DIM), 1.0),
        'state_b_ssm': nrm((N_LAYERS_B, DEC_BATCH, DN_V_HEADS, DN_HEAD_DIM, DN_HEAD_DIM), 0.1),
        'cache_c_k': nrm((N_LAYERS_C, DEC_BATCH, c_len, ATT_HEADS, ATT_HEAD_DIM), 1.0),
        'cache_c_v': nrm((N_LAYERS_C, DEC_BATCH, c_len, ATT_HEADS, ATT_HEAD_DIM), 1.0),
        'state_ffn_conv': nrm((DEPTH, DEC_BATCH, FFN_CONV - 1, 2 * D_FF), 1.0),
        'p_prompt': nrm((DEPTH, BATCH, SEQ, PLE_DIM), 1.0),
        'p_sample': nrm((DEPTH, DEC_BATCH, DEC_SEQ, PLE_DIM), 1.0),
        'g_mix': gain((DEPTH, D_MODEL)),
        'g_ffn': gain((DEPTH, D_MODEL)),
        'w_a_in': nrm((N_LAYERS_A, D_MODEL, 3 * D_MODEL), D_MODEL ** -0.5),
        'w_a_conv': nrm((N_LAYERS_A, A_CONV, D_MODEL), A_CONV ** -0.5),
        'w_a_out': nrm((N_LAYERS_A, D_MODEL, D_MODEL), D_MODEL ** -0.5 * out_scale),
        'w_b_in': nrm((N_LAYERS_B, D_MODEL, DN_IN_DIM), D_MODEL ** -0.5),
        'w_b_conv': nrm((N_LAYERS_B, DN_CONV, DN_CONV_DIM), DN_CONV ** -0.5),
        'b_a_log': a_log,
        'b_dt_bias': dt_bias,
        'g_b_out': gain((N_LAYERS_B, DN_HEAD_DIM)),
        'w_b_out': nrm((N_LAYERS_B, DN_VAL_DIM, D_MODEL), DN_VAL_DIM ** -0.5 * out_scale),
        'w_c_qkv': nrm((N_LAYERS_C, D_MODEL, 3 * ATT_HEADS * ATT_HEAD_DIM), D_MODEL ** -0.5),
        'g_c_q': gain((N_LAYERS_C, ATT_HEAD_DIM)),
        'g_c_k': gain((N_LAYERS_C, ATT_HEAD_DIM)),
        'c_rel_bias': nrm((N_LAYERS_C, ATT_HEADS, 2 * REL_CLIP + 1), 0.2),
        'w_c_out': nrm((N_LAYERS_C, ATT_HEADS * ATT_HEAD_DIM, D_MODEL), D_MODEL ** -0.5 * out_scale),
        'w_f_up': nrm((DEPTH, D_MODEL, 2 * D_FF), D_MODEL ** -0.5),
        'w_f_conv': nrm((DEPTH, FFN_CONV, 2 * D_FF), FFN_CONV ** -0.5),
        'b_f_conv': nrm((DEPTH, 2 * D_FF), 0.02),
        'w_f_down': nrm((DEPTH, D_FF, D_MODEL), D_FF ** -0.5 * out_scale),
        'g_ple': gain((DEPTH, D_MODEL)),
        'w_ple_gate': nrm((DEPTH, D_MODEL, D_MODEL), D_MODEL ** -0.5),
        'w_ple_proj': nrm((DEPTH, PLE_DIM, D_MODEL), PLE_DIM ** -0.5 * out_scale),
        'g_final': gain((D_MODEL,)),
    }


def reference(x_prompt, x_sample, state_a_conv, state_b_conv, state_b_ssm, cache_c_k, cache_c_v, state_ffn_conv,
              p_prompt, p_sample, g_mix, g_ffn, w_a_in, w_a_conv, w_a_out, w_b_in, w_b_conv, b_a_log, b_dt_bias,
              g_b_out, w_b_out, w_c_qkv, g_c_q, g_c_k, c_rel_bias, w_c_out, w_f_up, w_f_conv, b_f_conv, w_f_down,
              g_ple, w_ple_gate, w_ple_proj, g_final):
    prm = dict(g_mix=g_mix, g_ffn=g_ffn, w_a_in=w_a_in, w_a_conv=w_a_conv, w_a_out=w_a_out, w_b_in=w_b_in,
               w_b_conv=w_b_conv, b_a_log=b_a_log, b_dt_bias=b_dt_bias, g_b_out=g_b_out, w_b_out=w_b_out,
               w_c_qkv=w_c_qkv, g_c_q=g_c_q, g_c_k=g_c_k, c_rel_bias=c_rel_bias, w_c_out=w_c_out, w_f_up=w_f_up,
               w_f_conv=w_f_conv, b_f_conv=b_f_conv, w_f_down=w_f_down, g_ple=g_ple, w_ple_gate=w_ple_gate,
               w_ple_proj=w_ple_proj, g_final=g_final)
    bp = x_prompt.shape[0]
    dt = x_prompt.dtype
    y_prompt, pa, pbc, pbs, pk, pv, pf = run_trunk(
        x_prompt, p_prompt,
        jnp.zeros((N_LAYERS_A, bp, A_CONV - 1, D_MODEL), dt),
        jnp.zeros((N_LAYERS_B, bp, DN_CONV - 1, DN_CONV_DIM), dt),
        jnp.zeros((N_LAYERS_B, bp, DN_V_HEADS, DN_HEAD_DIM, DN_HEAD_DIM), dt),
        jnp.zeros((N_LAYERS_C, bp, 0, ATT_HEADS, ATT_HEAD_DIM), dt),
        jnp.zeros((N_LAYERS_C, bp, 0, ATT_HEADS, ATT_HEAD_DIM), dt),
        jnp.zeros((DEPTH, bp, FFN_CONV - 1, 2 * D_FF), dt),
        prm)
    y_sample, sa, sbc, sbs, sk, sv, sf = run_trunk(
        x_sample, p_sample, state_a_conv, state_b_conv, state_b_ssm, cache_c_k, cache_c_v, state_ffn_conv, prm)
    return (y_prompt, y_sample, pa, pbc, pbs, pk, pv, pf, sa, sbc, sbs, sk, sv, sf)
```

```python
import functools

import jax
import jax.numpy as jnp
from jax import lax
from jax.experimental import pallas as pl
from jax.experimental.pallas import tpu as pltpu

F32 = jnp.float32
BF16 = jnp.bfloat16

EPS = 1e-6
CHUNK = 64
LEFT_CHUNKS = 8
WINDOW = LEFT_CHUNKS * CHUNK
BAND = WINDOW + CHUNK
NEG_INF = -1e30

LANES = 128
SUBLANES = 8
HALO = SUBLANES
VMEM_LIMIT_BYTES = 56 * 1024 * 1024


def _params(n_axes):
    return pltpu.CompilerParams(dimension_semantics=("arbitrary",) * n_axes,
                                vmem_limit_bytes=VMEM_LIMIT_BYTES)


def _rms(x, g):
    return x * lax.rsqrt(jnp.mean(x * x, axis=-1, keepdims=True) + EPS) * g


def _silu(x):
    return x * jax.nn.sigmoid(x)


def _dot(a, b):
    return jnp.dot(a.astype(BF16), b.astype(BF16), preferred_element_type=F32)


def _dot_nt(a, b):
    return lax.dot_general(a.astype(BF16), b.astype(BF16), (((1,), (1,)), ((), ())),
                           preferred_element_type=F32)


def _largest_tile(n, target, quantum):
    if n <= target:
        return n
    t = (target // quantum) * quantum
    while t > quantum and n % t:
        t -= quantum
    assert n % t == 0, (n, target, quantum)
    return t


def _seq_tiles(bsz, seq, target=512):
    if seq >= target:
        return 1, _largest_tile(seq, target, CHUNK)
    bb = max(1, min(bsz, (2 * target) // seq))
    while bsz % bb:
        bb -= 1
    return bb, seq


def _pad_hist(hist):
    b, r, c = hist.shape
    return jnp.concatenate([jnp.zeros((b, HALO - r, c), hist.dtype), hist], axis=1)


def _causal_conv(x3, xx_ref, carry_ref, hist_ref, nh_ref, wc_ref, width, ts, at_seq_start):
    @pl.when(at_seq_start)
    def _():
        xx_ref[:, 0:HALO, :] = hist_ref[...]

    @pl.when(jnp.logical_not(at_seq_start))
    def _():
        xx_ref[:, 0:HALO, :] = carry_ref[...]

    xx_ref[:, HALO:, :] = x3
    carry_ref[...] = xx_ref[:, ts:ts + HALO, :]
    nh_ref[...] = xx_ref[:, ts + HALO - (width - 1):ts + HALO, :]
    y = x3 * wc_ref[width - 1:width, :][None]
    for t in range(width - 1):
        lo = HALO - (width - 1) + t
        y = y + xx_ref[:, lo:lo + ts, :] * wc_ref[t:t + 1, :][None]
    return y


def _conv_mlp_body(*refs, mode, width, bb, ts, tf, nf, d):
    n_up = 2 if mode == "ffn" else 3
    n_cv = 2 if mode == "ffn" else 1
    it = iter(refs)
    h_ref, g_ref = next(it), next(it)
    wu = [next(it) for _ in range(n_up)]
    wc = [next(it) for _ in range(n_cv)]
    bc = [next(it) for _ in range(n_cv)] if mode == "ffn" else None
    hist = [next(it) for _ in range(n_cv)]
    wd_ref = next(it)
    out_ref = next(it)
    nh = [next(it) for _ in range(n_cv)]
    hn_s, acc_s, xx_s, carry_s = next(it), next(it), next(it), next(it)

    s = pl.program_id(1)
    j = pl.program_id(2)
    m = bb * ts

    @pl.when(j == 0)
    def _():
        x = h_ref[...].reshape(m, d)
        hn_s[...] = _rms(x, g_ref[...]).astype(BF16)
        acc_s[...] = x

    hn = hn_s[...]
    ups = [jnp.dot(hn, w[...], preferred_element_type=F32).reshape(bb, ts, tf) for w in wu]
    cin = ups if mode == "ffn" else [ups[1] * ups[2]]
    ys = []
    for cg in range(n_cv):
        y = _causal_conv(cin[cg], xx_s.at[cg], carry_s.at[j * n_cv + cg], hist[cg], nh[cg], wc[cg],
                         width, ts, s == 0)
        if bc is not None:
            y = y + bc[cg][...][None]
        ys.append(y)
    act = _silu(ys[0]) * ys[1] if mode == "ffn" else ups[0] * ys[0]
    acc_s[...] += jnp.dot(act.reshape(m, tf).astype(BF16), wd_ref[...], preferred_element_type=F32)

    @pl.when(j == nf - 1)
    def _():
        out_ref[...] = acc_s[...].reshape(bb, ts, d)


def _conv_mlp(h, gain, w_up, w_conv, b_conv, hist, w_down, *, mode, name):
    bsz, seq, d = h.shape
    f = w_down.shape[0]
    width = w_conv.shape[0]
    n_up = 2 if mode == "ffn" else 3
    n_cv = 2 if mode == "ffn" else 1
    bb, ts = _seq_tiles(bsz, seq)
    tf = _largest_tile(f, 512, LANES)
    nf = f // tf
    hist8 = _pad_hist(hist)

    in_specs = [pl.BlockSpec((bb, ts, d), lambda b, s, j: (b, s, 0)),
                pl.BlockSpec((1, d), lambda b, s, j: (0, 0))]
    args = [h, gain.reshape(1, d)]
    for g in range(n_up):
        in_specs.append(pl.BlockSpec((d, tf), lambda b, s, j, g=g: (0, g * nf + j)))
        args.append(w_up)
    for g in range(n_cv):
        in_specs.append(pl.BlockSpec((width, tf), lambda b, s, j, g=g: (0, g * nf + j)))
        args.append(w_conv)
    if mode == "ffn":
        for g in range(n_cv):
            in_specs.append(pl.BlockSpec((1, tf), lambda b, s, j, g=g: (0, g * nf + j)))
            args.append(b_conv.reshape(1, -1))
    for g in range(n_cv):
        in_specs.append(pl.BlockSpec((bb, HALO, tf), lambda b, s, j, g=g: (b, 0, g * nf + j)))
        args.append(hist8)
    in_specs.append(pl.BlockSpec((tf, d), lambda b, s, j: (j, 0)))
    args.append(w_down)

    out_shape = [jax.ShapeDtypeStruct((bsz, seq, d), F32)]
    out_specs = [pl.BlockSpec((bb, ts, d), lambda b, s, j: (b, s, 0))]
    for g in range(n_cv):
        out_shape.append(jax.ShapeDtypeStruct((bsz, seq // ts, width - 1, f), F32))
        out_specs.append(pl.BlockSpec((bb, None, width - 1, tf), lambda b, s, j: (b, s, 0, j)))

    body = functools.partial(_conv_mlp_body, mode=mode, width=width, bb=bb, ts=ts, tf=tf, nf=nf, d=d)
    outs = pl.pallas_call(
        body,
        grid=(bsz // bb, seq // ts, nf),
        in_specs=in_specs,
        out_specs=out_specs,
        out_shape=out_shape,
        scratch_shapes=[pltpu.VMEM((bb * ts, d), BF16),
                        pltpu.VMEM((bb * ts, d), F32),
                        pltpu.VMEM((n_cv, bb, HALO + ts, tf), F32),
                        pltpu.VMEM((nf * n_cv, bb, HALO, tf), F32)],
        compiler_params=_params(3),
        name=name,
    )(*args)
    return outs[0], jnp.concatenate([o[:, -1] for o in outs[1:]], axis=-1)


def _proj_body(*refs, width, bb, ts, tn, d, n_extra, epilogue):
    conv = width > 0
    it = iter(refs)
    h_ref, g_ref, w_ref = next(it), next(it), next(it)
    if conv:
        wc_ref, hist_ref = next(it), next(it)
    extra = [next(it) for _ in range(n_extra)]
    out_ref = next(it)
    if conv:
        nh_ref = next(it)
    hn_s = next(it)
    if conv:
        xx_s, carry_s = next(it), next(it)

    s = pl.program_id(1)
    j = pl.program_id(2)
    m = bb * ts

    @pl.when(j == 0)
    def _():
        hn_s[...] = _rms(h_ref[...].reshape(m, d), g_ref[...]).astype(BF16)

    y = jnp.dot(hn_s[...], w_ref[...], preferred_element_type=F32).reshape(bb, ts, tn)
    if conv:
        y = _causal_conv(y, xx_s, carry_s.at[j], hist_ref, nh_ref, wc_ref, width, ts, s == 0)
    epilogue(j, y, extra, out_ref)


def _norm_proj(h, gain, w, *, epilogue, extra=(), w_conv=None, hist=None, out_dtype=F32, col_group=None,
               name):
    bsz, seq, d = h.shape
    n = w.shape[1]
    conv = w_conv is not None
    width = w_conv.shape[0] if conv else 0
    bb, ts = _seq_tiles(bsz, seq)
    tn = _largest_tile(n if col_group is None else col_group, 512, LANES)
    nt = n // tn

    in_specs = [pl.BlockSpec((bb, ts, d), lambda b, s, j: (b, s, 0)),
                pl.BlockSpec((1, d), lambda b, s, j: (0, 0)),
                pl.BlockSpec((d, tn), lambda b, s, j: (0, j))]
    args = [h, gain.reshape(1, d), w]
    if conv:
        in_specs += [pl.BlockSpec((width, tn), lambda b, s, j: (0, j)),
                     pl.BlockSpec((bb, HALO, tn), lambda b, s, j: (b, 0, j))]
        args += [w_conv, _pad_hist(hist)]
    for e in extra:
        in_specs.append(pl.BlockSpec(e.shape, lambda b, s, j, nd=e.ndim: (0,) * nd))
        args.append(e)

    out_shape = [jax.ShapeDtypeStruct((bsz, seq, n), out_dtype)]
    out_specs = [pl.BlockSpec((bb, ts, tn), lambda b, s, j: (b, s, j))]
    scratch = [pltpu.VMEM((bb * ts, d), BF16)]
    if conv:
        out_shape.append(jax.ShapeDtypeStruct((bsz, seq // ts, width - 1, n), F32))
        out_specs.append(pl.BlockSpec((bb, None, width - 1, tn), lambda b, s, j: (b, s, 0, j)))
        scratch += [pltpu.VMEM((bb, HALO + ts, tn), F32), pltpu.VMEM((nt, bb, HALO, tn), F32)]

    body = functools.partial(_proj_body, width=width, bb=bb, ts=ts, tn=tn, d=d, n_extra=len(extra),
                             epilogue=epilogue)
    outs = pl.pallas_call(
        body,
        grid=(bsz // bb, seq // ts, nt),
        in_specs=in_specs,
        out_specs=out_specs,
        out_shape=out_shape,
        scratch_shapes=scratch,
        compiler_params=_params(3),
        name=name,
    )(*args)
    return (outs[0], outs[1][:, -1]) if conv else outs[0]


def _store_epilogue(j, y, extra, out_ref):
    out_ref[...] = y.astype(out_ref.dtype)


def _delta_qkv_epilogue(j, y, extra, out_ref, *, n_q, n_k, hd):
    y = _silu(y)
    heads = y.shape[-1] // hd

    def normed(scale):
        for i in range(heads):
            yh = y[:, :, i * hd:(i + 1) * hd]
            out_ref[:, :, i * hd:(i + 1) * hd] = (
                yh * lax.rsqrt(jnp.sum(yh * yh, axis=-1, keepdims=True) + EPS) * scale)

    @pl.when(j < n_q)
    def _():
        normed(hd ** -0.5)

    @pl.when(jnp.logical_and(j >= n_q, j < n_q + n_k))
    def _():
        normed(1.0)

    @pl.when(j >= n_q + n_k)
    def _():
        out_ref[...] = y


def _attn_qkv_epilogue(j, y, extra, out_ref, *, n_q, hd):
    gq_ref, gk_ref = extra
    heads = y.shape[-1] // hd

    def normed(g_ref):
        g = g_ref[...][None]
        for i in range(heads):
            out_ref[:, :, i * hd:(i + 1) * hd] = _rms(y[:, :, i * hd:(i + 1) * hd], g)

    @pl.when(j < n_q)
    def _():
        normed(gq_ref)

    @pl.when(jnp.logical_and(j >= n_q, j < 2 * n_q))
    def _():
        normed(gk_ref)

    @pl.when(j >= 2 * n_q)
    def _():
        out_ref[...] = y


def _out_proj_body(h_ref, x_ref, w_ref, out_ref):
    out_ref[...] = h_ref[...] + jnp.dot(x_ref[...], w_ref[...], preferred_element_type=F32)


def _out_proj(h, x, w, *, name):
    bsz, seq, d = h.shape
    k = x.shape[-1]
    t = bsz * seq
    tm = _largest_tile(t, 512, SUBLANES)
    tn = _largest_tile(d, 512, LANES)
    out = pl.pallas_call(
        _out_proj_body,
        grid=(t // tm, d // tn),
        in_specs=[pl.BlockSpec((tm, tn), lambda i, j: (i, j)),
                  pl.BlockSpec((tm, k), lambda i, j: (i, 0)),
                  pl.BlockSpec((k, tn), lambda i, j: (0, j))],
        out_specs=pl.BlockSpec((tm, tn), lambda i, j: (i, j)),
        out_shape=jax.ShapeDtypeStruct((t, d), F32),
        compiler_params=_params(2),
        name=name,
    )(h.reshape(t, d), x.reshape(t, k), w)
    return out.reshape(bsz, seq, d)


def _ple_body(*refs, final):
    if final:
        h_ref, p_ref, g_ref, wg_ref, wp_ref, gf_ref, out_ref, y_ref = refs
    else:
        h_ref, p_ref, g_ref, wg_ref, wp_ref, out_ref = refs
    x = h_ref[...]
    gate = jax.nn.sigmoid(_dot(_rms(x, g_ref[...]), wg_ref[...]))
    hn = x + _dot(p_ref[...], wp_ref[...]) * gate
    out_ref[...] = hn
    if final:
        y_ref[...] = _rms(hn, gf_ref[...])


def _ple(h, p, gain, w_gate, w_proj, g_final=None, *, name):
    bsz, seq, d = h.shape
    pd = p.shape[-1]
    t = bsz * seq
    tm = _largest_tile(t, 256, SUBLANES)
    final = g_final is not None
    in_specs = [pl.BlockSpec((tm, d), lambda i: (i, 0)),
                pl.BlockSpec((tm, pd), lambda i: (i, 0)),
                pl.BlockSpec((1, d), lambda i: (0, 0)),
                pl.BlockSpec((d, d), lambda i: (0, 0)),
                pl.BlockSpec((pd, d), lambda i: (0, 0))]
    args = [h.reshape(t, d), p.reshape(t, pd), gain.reshape(1, d), w_gate, w_proj]
    out_shape = [jax.ShapeDtypeStruct((t, d), F32)]
    out_specs = [pl.BlockSpec((tm, d), lambda i: (i, 0))]
    if final:
        in_specs.append(pl.BlockSpec((1, d), lambda i: (0, 0)))
        args.append(g_final.reshape(1, d))
        out_shape.append(jax.ShapeDtypeStruct((t, d), F32))
        out_specs.append(pl.BlockSpec((tm, d), lambda i: (i, 0)))
    outs = pl.pallas_call(
        functools.partial(_ple_body, final=final),
        grid=(t // tm,),
        in_specs=in_specs,
        out_specs=out_specs,
        out_shape=out_shape,
        compiler_params=_params(1),
        name=name,
    )(*args)
    if final:
        return outs[0].reshape(bsz, seq, d), outs[1].reshape(bsz, seq, d)
    return outs[0].reshape(bsz, seq, d), None


_INV_BASE_LOG2 = 3


def _unit_lower_inverse(mneg, eye, same_blk):
    c = mneg.shape[0]
    m = jnp.where(same_blk[0], mneg, 0.0)
    p = eye + m
    m2 = _dot(m, m)
    r = _dot(jnp.concatenate([m2, p], axis=0), m2)
    p = p + r[c:]
    x = p + _dot(r[:c], p)
    for lvl in range(1, len(same_blk)):
        off = jnp.where(jnp.logical_and(same_blk[lvl], jnp.logical_not(same_blk[lvl - 1])), mneg, 0.0)
        x = x + _dot(_dot(x, off), x)
    return x


def _delta_body(q_ref, k_ref, kt_ref, v_ref, z_ref, a_ref, b_ref, alog_ref, dtb_ref, gout_ref, s0_ref,
                o_ref, sout_ref, state_s, *, hb, rep, hd, n_chunks):
    c_blk = pl.program_id(2)
    C = CHUNK

    @pl.when(c_blk == 0)
    def _():
        state_s[...] = s0_ref[...]

    row = lax.broadcasted_iota(jnp.int32, (C, C), 0)
    col = lax.broadcasted_iota(jnp.int32, (C, C), 1)
    tril = col <= row
    triu = row <= col
    strict = col < row
    eye = (col == row).astype(F32)
    same_blk = [jnp.right_shift(row, s) == jnp.right_shift(col, s) for s in range(_INV_BASE_LOG2, 7)]
    neg_rate = -jnp.exp(alog_ref[...])
    dt_bias = dtb_ref[...]
    g_out = gout_ref[...]

    def chunk_step(ci, carry):
        r0 = pl.multiple_of(ci * C, C)
        a_c = a_ref[pl.ds(r0, C), :] + dt_bias
        g_all = neg_rate * (jnp.maximum(a_c, 0.0) + jnp.log1p(jnp.exp(-jnp.abs(a_c))))
        beta_all = jax.nn.sigmoid(b_ref[pl.ds(r0, C), :])
        for iq in range(hb // rep):
            q_c = q_ref[pl.ds(r0, C), iq * hd:(iq + 1) * hd]
            k_c = k_ref[pl.ds(r0, C), iq * hd:(iq + 1) * hd]
            kt_c = kt_ref[iq, ci]
            kq = _dot_nt(jnp.concatenate([k_c, q_c], axis=0), k_c)
            kk, qk = kq[:C], kq[C:]
            for iv in range(iq * rep, (iq + 1) * rep):
                g_col = g_all[:, iv:iv + 1]
                beta = beta_all[:, iv:iv + 1]
                gc_row = jnp.sum(jnp.where(triu, g_col, 0.0), axis=0, keepdims=True)
                gc_col = jnp.sum(eye * gc_row, axis=1, keepdims=True)
                decay = jnp.where(tril, jnp.exp(jnp.where(tril, gc_col - gc_row, 0.0)), 0.0)
                mneg = jnp.where(strict, -(beta * kk) * decay, 0.0)
                a_intra = jnp.where(tril, qk * decay, 0.0)
                tmat = _unit_lower_inverse(mneg, eye, same_blk)
                v_c = v_ref[pl.ds(r0, C), iv * hd:(iv + 1) * hd]
                e_col = jnp.exp(gc_col)
                kb = k_c * beta
                uw = _dot(tmat, jnp.concatenate([v_c * beta, kb * e_col], axis=1))
                u, w = uw[:, :hd], uw[:, hd:]
                state = state_s[iv]
                ws_qs = _dot(jnp.concatenate([w, q_c * e_col], axis=0), state)
                v_new = u - ws_qs[:C]
                o = ws_qs[C:] + _dot(a_intra, v_new)
                g_last = gc_row[:, C - 1:C]
                state_s[iv] = state * jnp.exp(g_last) + _dot(kt_c * jnp.exp(g_last - gc_row), v_new)
                z_c = z_ref[pl.ds(r0, C), iv * hd:(iv + 1) * hd]
                o_ref[pl.ds(r0, C), iv * hd:(iv + 1) * hd] = (_rms(o, g_out) * _silu(z_c)).astype(o_ref.dtype)
        return carry

    lax.fori_loop(0, n_chunks, chunk_step, 0)

    @pl.when(c_blk == pl.num_programs(2) - 1)
    def _():
        sout_ref[...] = state_s[...]


def _delta_rule(qkv, z, ba, a_log, dt_bias, g_out, state0, *, kd, name):
    bsz, seq, _ = qkv.shape
    _, vh, hd, _ = state0.shape
    vd = vh * hd
    qh = kd // hd
    rep = vh // qh
    hb = min(vh, 2 * rep)
    qb = hb // rep
    ng = vh // hb
    cs = _largest_tile(seq, 512, CHUNK)
    n_chunks = cs // CHUNK
    nc = seq // CHUNK

    k = qkv[:, :, kd:2 * kd]
    kt = k.reshape(bsz, nc, CHUNK, qh, hd).transpose(0, 3, 1, 4, 2)
    b_cols = ba[:, :, :vh].reshape(bsz, seq, ng, hb).transpose(0, 2, 1, 3)
    a_cols = ba[:, :, vh:2 * vh].reshape(bsz, seq, ng, hb).transpose(0, 2, 1, 3)

    body = functools.partial(_delta_body, hb=hb, rep=rep, hd=hd, n_chunks=n_chunks)
    o, s_out = pl.pallas_call(
        body,
        grid=(bsz, ng, seq // cs),
        in_specs=[
            pl.BlockSpec((None, cs, qb * hd), lambda b, g, c: (b, c, g)),
            pl.BlockSpec((None, cs, qb * hd), lambda b, g, c: (b, c, kd // (qb * hd) + g)),
            pl.BlockSpec((None, qb, n_chunks, hd, CHUNK), lambda b, g, c: (b, g, c, 0, 0)),
            pl.BlockSpec((None, cs, hb * hd), lambda b, g, c: (b, c, 2 * kd // (hb * hd) + g)),
            pl.BlockSpec((None, cs, hb * hd), lambda b, g, c: (b, c, g)),
            pl.BlockSpec((None, None, cs, hb), lambda b, g, c: (b, g, c, 0)),
            pl.BlockSpec((None, None, cs, hb), lambda b, g, c: (b, g, c, 0)),
            pl.BlockSpec((None, 1, hb), lambda b, g, c: (g, 0, 0)),
            pl.BlockSpec((None, 1, hb), lambda b, g, c: (g, 0, 0)),
            pl.BlockSpec((1, hd), lambda b, g, c: (0, 0)),
            pl.BlockSpec((None, hb, hd, hd), lambda b, g, c: (b, g, 0, 0)),
        ],
        out_specs=[pl.BlockSpec((None, cs, hb * hd), lambda b, g, c: (b, c, g)),
                   pl.BlockSpec((None, hb, hd, hd), lambda b, g, c: (b, g, 0, 0))],
        out_shape=[jax.ShapeDtypeStruct((bsz, seq, vd), BF16),
                   jax.ShapeDtypeStruct((bsz, vh, hd, hd), F32)],
        scratch_shapes=[pltpu.VMEM((hb, hd, hd), F32)],
        compiler_params=_params(3),
        name=name,
    )(qkv, qkv, kt, qkv, z, a_cols, b_cols, a_log.reshape(ng, 1, hb), dt_bias.reshape(ng, 1, hb),
      g_out.reshape(1, hd), state0)
    return o, s_out


def _band_attn_body(q_ref, k_ref, v_ref, bias_ref, o_ref, *, n_chunks, n_hist, scale):
    bias = bias_ref[...]
    key = lax.broadcasted_iota(jnp.int32, (CHUNK, BAND), 1)

    def chunk_step(c, carry):
        r0 = pl.multiple_of(c * CHUNK, CHUNK)
        q_c = q_ref[pl.ds(r0, CHUNK), :]
        k_b = k_ref[pl.ds(r0, BAND), :]
        v_b = v_ref[pl.ds(r0, BAND), :]
        sc = _dot_nt(q_c, k_b) * scale + bias
        sc = jnp.where(key + r0 >= WINDOW - n_hist, sc, NEG_INF)
        sc = sc - jnp.max(sc, axis=-1, keepdims=True)
        e = jnp.exp(sc)
        p = e / jnp.sum(e, axis=-1, keepdims=True)
        o_ref[pl.ds(r0, CHUNK), :] = _dot(p, v_b).astype(o_ref.dtype)
        return carry

    lax.fori_loop(0, n_chunks, chunk_step, 0)


def _band_attention(q, k_ext, v_ext, bias, *, n_hist, name):
    bsz, seq, hdim = q.shape
    heads = bias.shape[0]
    hd = hdim // heads
    body = functools.partial(_band_attn_body, n_chunks=seq // CHUNK, n_hist=n_hist, scale=hd ** -0.5)
    return pl.pallas_call(
        body,
        grid=(bsz, heads),
        in_specs=[pl.BlockSpec((None, seq, hd), lambda b, h: (b, 0, h)),
                  pl.BlockSpec((None, WINDOW + seq, hd), lambda b, h: (b, 0, h)),
                  pl.BlockSpec((None, WINDOW + seq, hd), lambda b, h: (b, 0, h)),
                  pl.BlockSpec((None, CHUNK, BAND), lambda b, h: (h, 0, 0))],
        out_specs=pl.BlockSpec((None, seq, hd), lambda b, h: (b, 0, h)),
        out_shape=jax.ShapeDtypeStruct((bsz, seq, hdim), BF16),
        compiler_params=_params(2),
        name=name,
    )(q, k_ext, v_ext, bias)


def _pad_cols(w, mult):
    pad = (-w.shape[1]) % mult
    return jnp.pad(w, ((0, 0), (0, pad))) if pad else w


def _run_trunk(x, p, conv_a, conv_b, ssm_b, hist_k, hist_v, conv_f, prm, tag):
    h = x
    bsz, seq, d = x.shape
    depth = prm["g_mix"].shape[0]
    keep = min(WINDOW, seq)
    out_a, out_bc, out_bs, out_k, out_v, out_f = [], [], [], [], [], []
    y = None
    for i in range(depth):
        kind, j = i % 3, i // 3
        if kind == 0:
            h, st = _conv_mlp(h, prm["g_mix"][i], prm["w_a_in"][j], prm["w_a_conv"][j], None, conv_a[j],
                              prm["w_a_out"][j], mode="a", name=f"{tag}_mixer_a{i}")
            out_a.append(st)
        elif kind == 1:
            vh, hd = ssm_b.shape[2], ssm_b.shape[3]
            vd = vh * hd
            conv_dim = prm["w_b_conv"].shape[-1]
            kd = (conv_dim - vd) // 2
            w_in = prm["w_b_in"][j]
            tn = _largest_tile(kd, 512, LANES)
            epi = functools.partial(_delta_qkv_epilogue, n_q=kd // tn, n_k=kd // tn, hd=hd)
            qkv, st_c = _norm_proj(h, prm["g_mix"][i], w_in[:, :conv_dim], epilogue=epi, col_group=kd,
                                   w_conv=prm["w_b_conv"][j], hist=conv_b[j], name=f"{tag}_delta_qkv{i}")
            z = _norm_proj(h, prm["g_mix"][i], w_in[:, conv_dim:conv_dim + vd], epilogue=_store_epilogue,
                           name=f"{tag}_delta_z{i}")
            ba = _norm_proj(h, prm["g_mix"][i], _pad_cols(w_in[:, conv_dim + vd:], LANES),
                            epilogue=_store_epilogue, name=f"{tag}_delta_ba{i}")
            o, st_s = _delta_rule(qkv, z, ba, prm["b_a_log"][j], prm["b_dt_bias"][j], prm["g_b_out"][j],
                                  ssm_b[j], kd=kd, name=f"{tag}_delta_rule{i}")
            h = _out_proj(h, o, prm["w_b_out"][j], name=f"{tag}_delta_out{i}")
            out_bc.append(st_c)
            out_bs.append(st_s)
        else:
            heads, hd = hist_k.shape[-2], hist_k.shape[-1]
            hdim = heads * hd
            n_hist = hist_k.shape[2]
            tn = _largest_tile(hdim, 512, LANES)
            epi = functools.partial(_attn_qkv_epilogue, n_q=hdim // tn, hd=hd)
            qkv = _norm_proj(h, prm["g_mix"][i], prm["w_c_qkv"][j], epilogue=epi, col_group=hdim,
                             extra=(prm["g_c_q"][j].reshape(1, hd), prm["g_c_k"][j].reshape(1, hd)),
                             name=f"{tag}_attn_qkv{i}")
            q, k_new, v_new = qkv[:, :, :hdim], qkv[:, :, hdim:2 * hdim], qkv[:, :, 2 * hdim:]

            def extend(new, hist):
                return jnp.concatenate([jnp.zeros((bsz, WINDOW - n_hist, hdim), F32),
                                        hist.reshape(bsz, n_hist, hdim), new], axis=1)

            rel_bias = prm["c_rel_bias"][j]
            clip = (rel_bias.shape[-1] - 1) // 2
            rel = WINDOW + jnp.arange(CHUNK)[:, None] - jnp.arange(BAND)[None, :]
            bias = rel_bias[:, jnp.clip(rel, -clip, clip) + clip]
            o = _band_attention(q, extend(k_new, hist_k[j]), extend(v_new, hist_v[j]), bias,
                                n_hist=n_hist, name=f"{tag}_attn{i}")
            h = _out_proj(h, o, prm["w_c_out"][j], name=f"{tag}_attn_out{i}")
            out_k.append(k_new[:, seq - keep:].reshape(bsz, keep, heads, hd))
            out_v.append(v_new[:, seq - keep:].reshape(bsz, keep, heads, hd))
        h, st = _conv_mlp(h, prm["g_ffn"][i], prm["w_f_up"][i], prm["w_f_conv"][i], prm["b_f_conv"][i],
                          conv_f[i], prm["w_f_down"][i], mode="ffn", name=f"{tag}_ffn{i}")
        out_f.append(st)
        h, y = _ple(h, p[i], prm["g_ple"][i], prm["w_ple_gate"][i], prm["w_ple_proj"][i],
                    prm["g_final"] if i == depth - 1 else None, name=f"{tag}_ple{i}")
    return (y, jnp.stack(out_a), jnp.stack(out_bc), jnp.stack(out_bs), jnp.stack(out_k), jnp.stack(out_v),
            jnp.stack(out_f))


def kernel(x_prompt, x_sample, state_a_conv, state_b_conv, state_b_ssm, cache_c_k, cache_c_v, state_ffn_conv,
           p_prompt, p_sample, g_mix, g_ffn, w_a_in, w_a_conv, w_a_out, w_b_in, w_b_conv, b_a_log, b_dt_bias,
           g_b_out, w_b_out, w_c_qkv, g_c_q, g_c_k, c_rel_bias, w_c_out, w_f_up, w_f_conv, b_f_conv, w_f_down,
           g_ple, w_ple_gate, w_ple_proj, g_final):
    prm = dict(g_mix=g_mix, g_ffn=g_ffn, w_a_in=w_a_in.astype(BF16), w_a_conv=w_a_conv,
               w_a_out=w_a_out.astype(BF16), w_b_in=w_b_in.astype(BF16), w_b_conv=w_b_conv, b_a_log=b_a_log,
               b_dt_bias=b_dt_bias, g_b_out=g_b_out, w_b_out=w_b_out.astype(BF16), w_c_qkv=w_c_qkv.astype(BF16),
               g_c_q=g_c_q, g_c_k=g_c_k, c_rel_bias=c_rel_bias, w_c_out=w_c_out.astype(BF16),
               w_f_up=w_f_up.astype(BF16), w_f_conv=w_f_conv, b_f_conv=b_f_conv, w_f_down=w_f_down.astype(BF16),
               g_ple=g_ple, w_ple_gate=w_ple_gate.astype(BF16), w_ple_proj=w_ple_proj.astype(BF16),
               g_final=g_final)
    bp = x_prompt.shape[0]
    dt = x_prompt.dtype
    zeros_like_state = lambda s: jnp.zeros((s.shape[0], bp) + s.shape[2:], dt)
    heads, hd = cache_c_k.shape[-2:]
    empty_kv = jnp.zeros((cache_c_k.shape[0], bp, 0, heads, hd), dt)
    prompt = _run_trunk(x_prompt, p_prompt, zeros_like_state(state_a_conv), zeros_like_state(state_b_conv),
                        zeros_like_state(state_b_ssm), empty_kv, empty_kv, zeros_like_state(state_ffn_conv),
                        prm, "prompt")
    sample = _run_trunk(x_sample, p_sample, state_a_conv, state_b_conv, state_b_ssm, cache_c_k, cache_c_v,
                        state_ffn_conv, prm, "sample")
    return (prompt[0], sample[0]) + prompt[1:] + sample[1:]
```

```python
import functools

import jax
import jax.numpy as jnp
from jax import lax
from jax.experimental import pallas as pl
from jax.experimental.pallas import tpu as pltpu

F32 = jnp.float32
BF16 = jnp.bfloat16

EPS = 1e-6
CHUNK = 64
LEFT_CHUNKS = 8
WINDOW = LEFT_CHUNKS * CHUNK
BAND = WINDOW + CHUNK
NEG_INF = -1e30

LANES = 128
SUBLANES = 8
HALO = SUBLANES
VMEM_LIMIT_BYTES = 56 * 1024 * 1024


def _params(n_axes):
    return pltpu.CompilerParams(dimension_semantics=("arbitrary",) * n_axes,
                                vmem_limit_bytes=VMEM_LIMIT_BYTES)


def _rms(x, g):
    return x * lax.rsqrt(jnp.mean(x * x, axis=-1, keepdims=True) + EPS) * g


def _silu(x):
    return x * jax.nn.sigmoid(x)


def _dot(a, b):
    return jnp.dot(a.astype(BF16), b.astype(BF16), preferred_element_type=F32)


def _dot_nt(a, b):
    return lax.dot_general(a.astype(BF16), b.astype(BF16), (((1,), (1,)), ((), ())),
                           preferred_element_type=F32)


def _largest_tile(n, target, quantum):
    if n <= target:
        return n
    t = (target // quantum) * quantum
    while t > quantum and n % t:
        t -= quantum
    assert n % t == 0, (n, target, quantum)
    return t


MXU_WIDTH = 256


def _proj_tile(n):
    return _largest_tile(n, 4 * MXU_WIDTH, LANES)


def _n_sub(tile):
    return tile // MXU_WIDTH if tile % MXU_WIDTH == 0 else 1


def _seq_tiles(bsz, seq, target=512):
    if seq >= target:
        return 1, _largest_tile(seq, target, CHUNK)
    bb = max(1, min(bsz, target // seq))
    while bsz % bb:
        bb -= 1
    return bb, seq


def _pad_hist(hist):
    b, r, c = hist.shape
    return jnp.concatenate([jnp.zeros((b, HALO - r, c), hist.dtype), hist], axis=1)


def _causal_conv(x3, xx_ref, carry_ref, hist_ref, nh_ref, wc_ref, width, ts, at_seq_start):
    xx_ref[:, 0:HALO, :] = jnp.where(at_seq_start, hist_ref[...], carry_ref[...])
    xx_ref[:, HALO:, :] = x3
    carry_ref[...] = xx_ref[:, ts:ts + HALO, :]
    nh_ref[...] = xx_ref[:, ts + HALO - (width - 1):ts + HALO, :]
    y = x3 * wc_ref[width - 1:width, :][None]
    for t in range(width - 1):
        lo = HALO - (width - 1) + t
        y = y + xx_ref[:, lo:lo + ts, :] * wc_ref[t:t + 1, :][None]
    return y


def _conv_mlp_body(*refs, mode, width, bb, ts, tf, nf, d, n_sub):
    n_up = 2 if mode == "ffn" else 3
    n_cv = 2 if mode == "ffn" else 1
    it = iter(refs)
    h_ref, g_ref = next(it), next(it)
    wu = [next(it) for _ in range(n_up)]
    wc = [next(it) for _ in range(n_cv)]
    bc = [next(it) for _ in range(n_cv)] if mode == "ffn" else None
    hist = [next(it) for _ in range(n_cv)]
    wd_ref = next(it)
    out_ref = next(it)
    nh = [next(it) for _ in range(n_cv)]
    hn_s, acc_s, xx_s, carry_s = next(it), next(it), next(it), next(it)

    s = pl.program_id(1)
    j = pl.program_id(2)
    m = bb * ts

    @pl.when(jnp.logical_and(jnp.logical_and(pl.program_id(0) == 0, s == 0), j == 0))
    def _():
        carry_s[...] = jnp.zeros_like(carry_s)

    @pl.when(j == 0)
    def _():
        x = h_ref[...].reshape(m, d)
        hn_s[...] = _rms(x, g_ref[...]).astype(BF16)
        acc_s[...] = x

    tsub = tf // n_sub
    hn = hn_s[...]
    all_ups = [[jnp.dot(hn, w[:, pl.ds(sub * tsub, tsub)], preferred_element_type=F32).reshape(bb, ts, tsub)
                for w in wu] for sub in range(n_sub)]
    for sub in range(n_sub):
        cols = pl.ds(sub * tsub, tsub)
        ups = all_ups[sub]
        cin = ups if mode == "ffn" else [ups[1] * ups[2]]
        ys = []
        for cg in range(n_cv):
            y = _causal_conv(cin[cg], xx_s.at[cg, :, :, cols], carry_s.at[j * n_cv + cg, :, :, cols],
                             hist[cg].at[:, :, cols], nh[cg].at[:, :, cols], wc[cg].at[:, cols],
                             width, ts, s == 0)
            if bc is not None:
                y = y + bc[cg][:, cols][None]
            ys.append(y)
        act = _silu(ys[0]) * ys[1] if mode == "ffn" else ups[0] * ys[0]
        acc_s[...] += jnp.dot(act.reshape(m, tsub).astype(BF16), wd_ref[cols, :], preferred_element_type=F32)

    @pl.when(j == nf - 1)
    def _():
        out_ref[...] = acc_s[...].reshape(bb, ts, d)


def _conv_mlp(h, gain, w_up, w_conv, b_conv, hist, w_down, *, mode, name):
    bsz, seq, d = h.shape
    f = w_down.shape[0]
    width = w_conv.shape[0]
    n_up = 2 if mode == "ffn" else 3
    n_cv = 2 if mode == "ffn" else 1
    bb, ts = _seq_tiles(bsz, seq)
    tf = _largest_tile(f, 512, LANES)
    nf = f // tf
    hist8 = _pad_hist(hist)

    in_specs = [pl.BlockSpec((bb, ts, d), lambda b, s, j: (b, s, 0)),
                pl.BlockSpec((1, d), lambda b, s, j: (0, 0))]
    args = [h, gain.reshape(1, d)]
    for g in range(n_up):
        in_specs.append(pl.BlockSpec((d, tf), lambda b, s, j, g=g: (0, g * nf + j)))
        args.append(w_up)
    for g in range(n_cv):
        in_specs.append(pl.BlockSpec((width, tf), lambda b, s, j, g=g: (0, g * nf + j)))
        args.append(w_conv)
    if mode == "ffn":
        for g in range(n_cv):
            in_specs.append(pl.BlockSpec((1, tf), lambda b, s, j, g=g: (0, g * nf + j)))
            args.append(b_conv.reshape(1, -1))
    for g in range(n_cv):
        in_specs.append(pl.BlockSpec((bb, HALO, tf), lambda b, s, j, g=g: (b, 0, g * nf + j)))
        args.append(hist8)
    in_specs.append(pl.BlockSpec((tf, d), lambda b, s, j: (j, 0)))
    args.append(w_down)

    out_shape = [jax.ShapeDtypeStruct((bsz, seq, d), F32)]
    out_specs = [pl.BlockSpec((bb, ts, d), lambda b, s, j: (b, s, 0))]
    for g in range(n_cv):
        out_shape.append(jax.ShapeDtypeStruct((bsz, seq // ts, width - 1, f), F32))
        out_specs.append(pl.BlockSpec((bb, None, width - 1, tf), lambda b, s, j: (b, s, 0, j)))

    n_sub = _n_sub(tf)
    body = functools.partial(_conv_mlp_body, mode=mode, width=width, bb=bb, ts=ts, tf=tf, nf=nf, d=d,
                             n_sub=n_sub)
    outs = pl.pallas_call(
        body,
        grid=(bsz // bb, seq // ts, nf),
        in_specs=in_specs,
        out_specs=out_specs,
        out_shape=out_shape,
        scratch_shapes=[pltpu.VMEM((bb * ts, d), BF16),
                        pltpu.VMEM((bb * ts, d), F32),
                        pltpu.VMEM((n_cv, bb, HALO + ts, tf), F32),
                        pltpu.VMEM((nf * n_cv, bb, HALO, tf), F32)],
        compiler_params=_params(3),
        name=name,
    )(*args)
    return outs[0], jnp.concatenate([o[:, -1] for o in outs[1:]], axis=-1)


def _proj_body(*refs, width, bb, ts, tn, d, n_extra, epilogue, n_sub):
    conv = width > 0
    it = iter(refs)
    h_ref, g_ref, w_ref = next(it), next(it), next(it)
    if conv:
        wc_ref, hist_ref = next(it), next(it)
    extra = [next(it) for _ in range(n_extra)]
    out_ref = next(it)
    if conv:
        nh_ref = next(it)
    hn_s = next(it)
    if conv:
        xx_s, carry_s = next(it), next(it)

    s = pl.program_id(1)
    j = pl.program_id(2)
    m = bb * ts

    if conv:
        @pl.when(jnp.logical_and(jnp.logical_and(pl.program_id(0) == 0, s == 0), j == 0))
        def _():
            carry_s[...] = jnp.zeros_like(carry_s)

    @pl.when(j == 0)
    def _():
        hn_s[...] = _rms(h_ref[...].reshape(m, d), g_ref[...]).astype(BF16)

    tsub = tn // n_sub
    hn = hn_s[...]
    ys = [jnp.dot(hn, w_ref[:, pl.ds(sub * tsub, tsub)], preferred_element_type=F32).reshape(bb, ts, tsub)
          for sub in range(n_sub)]
    for sub in range(n_sub):
        cols = pl.ds(sub * tsub, tsub)
        y = ys[sub]
        if conv:
            y = _causal_conv(y, xx_s.at[:, :, cols], carry_s.at[j, :, :, cols], hist_ref.at[:, :, cols],
                             nh_ref.at[:, :, cols], wc_ref.at[:, cols], width, ts, s == 0)
        epilogue(j, y, extra, out_ref.at[:, :, cols])


def _norm_proj(h, gain, w, *, epilogue, extra=(), w_conv=None, hist=None, out_dtype=F32, col_group=None,
               name):
    bsz, seq, d = h.shape
    n = w.shape[1]
    conv = w_conv is not None
    width = w_conv.shape[0] if conv else 0
    bb, ts = _seq_tiles(bsz, seq)
    tn = _proj_tile(n if col_group is None else col_group)
    nt = n // tn

    in_specs = [pl.BlockSpec((bb, ts, d), lambda b, s, j: (b, s, 0)),
                pl.BlockSpec((1, d), lambda b, s, j: (0, 0)),
                pl.BlockSpec((d, tn), lambda b, s, j: (0, j))]
    args = [h, gain.reshape(1, d), w]
    if conv:
        in_specs += [pl.BlockSpec((width, tn), lambda b, s, j: (0, j)),
                     pl.BlockSpec((bb, HALO, tn), lambda b, s, j: (b, 0, j))]
        args += [w_conv, _pad_hist(hist)]
    for e in extra:
        in_specs.append(pl.BlockSpec(e.shape, lambda b, s, j, nd=e.ndim: (0,) * nd))
        args.append(e)

    out_shape = [jax.ShapeDtypeStruct((bsz, seq, n), out_dtype)]
    out_specs = [pl.BlockSpec((bb, ts, tn), lambda b, s, j: (b, s, j))]
    scratch = [pltpu.VMEM((bb * ts, d), BF16)]
    if conv:
        out_shape.append(jax.ShapeDtypeStruct((bsz, seq // ts, width - 1, n), F32))
        out_specs.append(pl.BlockSpec((bb, None, width - 1, tn), lambda b, s, j: (b, s, 0, j)))
        scratch += [pltpu.VMEM((bb, HALO + ts, tn), F32), pltpu.VMEM((nt, bb, HALO, tn), F32)]

    body = functools.partial(_proj_body, width=width, bb=bb, ts=ts, tn=tn, d=d, n_extra=len(extra),
                             epilogue=epilogue, n_sub=_n_sub(tn))
    outs = pl.pallas_call(
        body,
        grid=(bsz // bb, seq // ts, nt),
        in_specs=in_specs,
        out_specs=out_specs,
        out_shape=out_shape,
        scratch_shapes=scratch,
        compiler_params=_params(3),
        name=name,
    )(*args)
    return (outs[0], outs[1][:, -1]) if conv else outs[0]


def _store_epilogue(j, y, extra, out_ref):
    out_ref[...] = y.astype(out_ref.dtype)


def _delta_qkv_epilogue(j, y, extra, out_ref, *, n_q, n_k, hd):
    y = _silu(y)
    heads = y.shape[-1] // hd
    q_scale = jnp.where(j < n_q, hd ** -0.5, 1.0)
    is_qk = j < n_q + n_k
    for i in range(heads):
        yh = y[:, :, i * hd:(i + 1) * hd]
        inv = lax.rsqrt(jnp.sum(yh * yh, axis=-1, keepdims=True) + EPS) * q_scale
        out_ref[:, :, i * hd:(i + 1) * hd] = yh * jnp.where(is_qk, inv, 1.0)


def _attn_qkv_epilogue(j, y, extra, out_ref, *, n_q, hd):
    gq_ref, gk_ref = extra
    heads = y.shape[-1] // hd
    g = jnp.where(j < n_q, gq_ref[...], gk_ref[...])[None]
    is_qk = j < 2 * n_q
    for i in range(heads):
        yh = y[:, :, i * hd:(i + 1) * hd]
        out_ref[:, :, i * hd:(i + 1) * hd] = jnp.where(is_qk, _rms(yh, g), yh)


def _out_proj_body(h_ref, x_ref, w_ref, out_ref):
    out_ref[...] = h_ref[...] + jnp.dot(x_ref[...], w_ref[...], preferred_element_type=F32)


def _out_proj(h, x, w, *, name):
    bsz, seq, d = h.shape
    k = x.shape[-1]
    t = bsz * seq
    tm = _largest_tile(t, 512, SUBLANES)
    tn = _largest_tile(d, 512, LANES)
    out = pl.pallas_call(
        _out_proj_body,
        grid=(t // tm, d // tn),
        in_specs=[pl.BlockSpec((tm, tn), lambda i, j: (i, j)),
                  pl.BlockSpec((tm, k), lambda i, j: (i, 0)),
                  pl.BlockSpec((k, tn), lambda i, j: (0, j))],
        out_specs=pl.BlockSpec((tm, tn), lambda i, j: (i, j)),
        out_shape=jax.ShapeDtypeStruct((t, d), F32),
        compiler_params=_params(2),
        name=name,
    )(h.reshape(t, d), x.reshape(t, k), w)
    return out.reshape(bsz, seq, d)


def _ple_body(*refs, final):
    if final:
        h_ref, p_ref, g_ref, wg_ref, wp_ref, gf_ref, out_ref, y_ref = refs
    else:
        h_ref, p_ref, g_ref, wg_ref, wp_ref, out_ref = refs
    x = h_ref[...]
    gate = jax.nn.sigmoid(_dot(_rms(x, g_ref[...]), wg_ref[...]))
    hn = x + _dot(p_ref[...], wp_ref[...]) * gate
    out_ref[...] = hn
    if final:
        y_ref[...] = _rms(hn, gf_ref[...])


def _ple(h, p, gain, w_gate, w_proj, g_final=None, *, name):
    bsz, seq, d = h.shape
    pd = p.shape[-1]
    t = bsz * seq
    tm = _largest_tile(t, 256, SUBLANES)
    final = g_final is not None
    in_specs = [pl.BlockSpec((tm, d), lambda i: (i, 0)),
                pl.BlockSpec((tm, pd), lambda i: (i, 0)),
                pl.BlockSpec((1, d), lambda i: (0, 0)),
                pl.BlockSpec((d, d), lambda i: (0, 0)),
                pl.BlockSpec((pd, d), lambda i: (0, 0))]
    args = [h.reshape(t, d), p.reshape(t, pd), gain.reshape(1, d), w_gate, w_proj]
    out_shape = [jax.ShapeDtypeStruct((t, d), F32)]
    out_specs = [pl.BlockSpec((tm, d), lambda i: (i, 0))]
    if final:
        in_specs.append(pl.BlockSpec((1, d), lambda i: (0, 0)))
        args.append(g_final.reshape(1, d))
        out_shape.append(jax.ShapeDtypeStruct((t, d), F32))
        out_specs.append(pl.BlockSpec((tm, d), lambda i: (i, 0)))
    outs = pl.pallas_call(
        functools.partial(_ple_body, final=final),
        grid=(t // tm,),
        in_specs=in_specs,
        out_specs=out_specs,
        out_shape=out_shape,
        compiler_params=_params(1),
        name=name,
    )(*args)
    if final:
        return outs[0].reshape(bsz, seq, d), outs[1].reshape(bsz, seq, d)
    return outs[0].reshape(bsz, seq, d), None


_INV_BASE_LOG2 = 3


def _bdot(a, b):
    return lax.dot_general(a.astype(BF16), b.astype(BF16), (((2,), (1,)), ((0,), (0,))),
                           preferred_element_type=F32)


def _bdot_nt(a, b):
    return lax.dot_general(a.astype(BF16), b.astype(BF16), (((2,), (2,)), ((0,), (0,))),
                           preferred_element_type=F32)


def _unit_lower_inverse(mneg, eye, same_blk):
    c = mneg.shape[1]
    m = jnp.where(same_blk[0], mneg, 0.0)
    p = eye + m
    m2 = _bdot(m, m)
    r = _bdot(jnp.concatenate([m2, p], axis=1), m2)
    p = p + r[:, c:]
    x = p + _bdot(r[:, :c], p)
    for lvl in range(1, len(same_blk)):
        off = jnp.where(jnp.logical_and(same_blk[lvl], jnp.logical_not(same_blk[lvl - 1])), mneg, 0.0)
        x = x + _bdot(_bdot(x, off), x)
    return x


def _delta_body(q_ref, k_ref, kt_ref, v_ref, z_ref, a_ref, b_ref, alog_ref, dtb_ref, gout_ref, s0_ref,
                o_ref, sout_ref, state_s, u_s, wq_s, a_s, kd_s, dl_s, *, hb, rep, hd, n_chunks, pb):
    c_blk = pl.program_id(2)
    C = CHUNK
    qb = hb // rep
    G = pb * hb

    @pl.when(c_blk == 0)
    def _():
        state_s[...] = s0_ref[...]

    row = lax.broadcasted_iota(jnp.int32, (C, C), 0)
    col = lax.broadcasted_iota(jnp.int32, (C, C), 1)
    tril = col <= row
    triu = row <= col
    strict = col < row
    eye = (col == row).astype(F32)
    same_blk = [jnp.right_shift(row, s) == jnp.right_shift(col, s) for s in range(_INV_BASE_LOG2, 7)]
    neg_rate = -jnp.exp(alog_ref[...])
    dt_bias = dtb_ref[...]
    g_out = gout_ref[...]

    def per_value_head(x):
        return jnp.stack([x[(i // hb) * qb + (i % hb) // rep] for i in range(G)])

    def phase1(step, carry):
        base = step * pb
        ks, qs, kts, g_cols, betas, vs = [], [], [], [], [], []
        for p in range(pb):
            r0 = pl.multiple_of((base + p) * C, C)
            a_c = a_ref[pl.ds(r0, C), :] + dt_bias
            g_all = neg_rate * (jnp.maximum(a_c, 0.0) + jnp.log1p(jnp.exp(-jnp.abs(a_c))))
            beta_all = jax.nn.sigmoid(b_ref[pl.ds(r0, C), :])
            for iq in range(qb):
                ks.append(k_ref[pl.ds(r0, C), iq * hd:(iq + 1) * hd])
                qs.append(q_ref[pl.ds(r0, C), iq * hd:(iq + 1) * hd])
                kts.append(kt_ref[iq, base + p])
            for iv in range(hb):
                g_cols.append(g_all[:, iv:iv + 1])
                betas.append(beta_all[:, iv:iv + 1])
                vs.append(v_ref[pl.ds(r0, C), iv * hd:(iv + 1) * hd])
        k_st, q_st = jnp.stack(ks), jnp.stack(qs)
        kq = _bdot_nt(jnp.concatenate([k_st, q_st], axis=1), k_st)
        kk, qk = per_value_head(kq[:, :C]), per_value_head(kq[:, C:])
        k_g, q_g, kt_g = per_value_head(k_st), per_value_head(q_st), per_value_head(jnp.stack(kts))
        g_col, beta, v_g = jnp.stack(g_cols), jnp.stack(betas), jnp.stack(vs)

        gc_row = jnp.sum(jnp.where(triu, g_col, 0.0), axis=1, keepdims=True)
        gc_col = jnp.sum(eye * gc_row, axis=2, keepdims=True)
        decay = jnp.where(tril, jnp.exp(jnp.where(tril, gc_col - gc_row, 0.0)), 0.0)
        mneg = jnp.where(strict, -(beta * kk) * decay, 0.0)
        a_intra = jnp.where(tril, qk * decay, 0.0)
        tmat = _unit_lower_inverse(mneg, eye, same_blk)
        e_col = jnp.exp(gc_col)
        uw = _bdot(tmat, jnp.concatenate([v_g * beta, k_g * (beta * e_col)], axis=2))
        g_last = gc_row[:, :, C - 1:C]
        kd = kt_g * jnp.exp(g_last - gc_row)
        wq = jnp.concatenate([uw[:, :, hd:], q_g * e_col], axis=1)

        u_s[pl.ds(base, pb)] = uw[:, :, :hd].reshape(pb, hb, C, hd)
        wq_s[pl.ds(base, pb)] = wq.astype(BF16).reshape(pb, hb, 2 * C, hd)
        a_s[pl.ds(base, pb)] = a_intra.astype(BF16).reshape(pb, hb, C, C)
        kd_s[pl.ds(base, pb)] = kd.astype(BF16).reshape(pb, hb, hd, C)
        dl_s[pl.ds(base, pb)] = jnp.broadcast_to(jnp.exp(g_last), (G, SUBLANES, hd)).reshape(pb, hb, SUBLANES, hd)
        return carry

    lax.fori_loop(0, n_chunks // pb, phase1, 0)

    def phase2(ci, carry):
        r0 = pl.multiple_of(ci * C, C)
        state = state_s[...]
        ws_qs = _bdot(wq_s[ci], state)
        v_new = u_s[ci] - ws_qs[:, :C]
        o = ws_qs[:, C:] + _bdot(a_s[ci], v_new)
        state_s[...] = state * dl_s[ci][:, 0:1, :] + _bdot(kd_s[ci], v_new)
        o = _rms(o, g_out)
        for iv in range(hb):
            z_c = z_ref[pl.ds(r0, C), iv * hd:(iv + 1) * hd]
            o_ref[pl.ds(r0, C), iv * hd:(iv + 1) * hd] = (o[iv] * _silu(z_c)).astype(o_ref.dtype)
        return carry

    lax.fori_loop(0, n_chunks, phase2, 0)

    @pl.when(c_blk == pl.num_programs(2) - 1)
    def _():
        sout_ref[...] = state_s[...]


def _delta_rule(qkv, z, ba, a_log, dt_bias, g_out, state0, *, kd, name):
    bsz, seq, _ = qkv.shape
    _, vh, hd, _ = state0.shape
    vd = vh * hd
    qh = kd // hd
    rep = vh // qh
    hb = min(vh, 4 * rep)
    qb = hb // rep
    ng = vh // hb
    cs = _largest_tile(seq, 4 * CHUNK, CHUNK)
    n_chunks = cs // CHUNK
    nc = seq // CHUNK
    pb = 2 if n_chunks % 2 == 0 else 1

    k = qkv[:, :, kd:2 * kd]
    kt = k.reshape(bsz, nc, CHUNK, qh, hd).transpose(0, 3, 1, 4, 2)
    b_cols = ba[:, :, :vh].reshape(bsz, seq, ng, hb).transpose(0, 2, 1, 3)
    a_cols = ba[:, :, vh:2 * vh].reshape(bsz, seq, ng, hb).transpose(0, 2, 1, 3)

    body = functools.partial(_delta_body, hb=hb, rep=rep, hd=hd, n_chunks=n_chunks, pb=pb)
    o, s_out = pl.pallas_call(
        body,
        grid=(bsz, ng, seq // cs),
        in_specs=[
            pl.BlockSpec((None, cs, qb * hd), lambda b, g, c: (b, c, g)),
            pl.BlockSpec((None, cs, qb * hd), lambda b, g, c: (b, c, kd // (qb * hd) + g)),
            pl.BlockSpec((None, qb, n_chunks, hd, CHUNK), lambda b, g, c: (b, g, c, 0, 0)),
            pl.BlockSpec((None, cs, hb * hd), lambda b, g, c: (b, c, 2 * kd // (hb * hd) + g)),
            pl.BlockSpec((None, cs, hb * hd), lambda b, g, c: (b, c, g)),
            pl.BlockSpec((None, None, cs, hb), lambda b, g, c: (b, g, c, 0)),
            pl.BlockSpec((None, None, cs, hb), lambda b, g, c: (b, g, c, 0)),
            pl.BlockSpec((None, 1, hb), lambda b, g, c: (g, 0, 0)),
            pl.BlockSpec((None, 1, hb), lambda b, g, c: (g, 0, 0)),
            pl.BlockSpec((1, hd), lambda b, g, c: (0, 0)),
            pl.BlockSpec((None, hb, hd, hd), lambda b, g, c: (b, g, 0, 0)),
        ],
        out_specs=[pl.BlockSpec((None, cs, hb * hd), lambda b, g, c: (b, c, g)),
                   pl.BlockSpec((None, hb, hd, hd), lambda b, g, c: (b, g, 0, 0))],
        out_shape=[jax.ShapeDtypeStruct((bsz, seq, vd), BF16),
                   jax.ShapeDtypeStruct((bsz, vh, hd, hd), F32)],
        scratch_shapes=[pltpu.VMEM((hb, hd, hd), F32),
                        pltpu.VMEM((n_chunks, hb, CHUNK, hd), F32),
                        pltpu.VMEM((n_chunks, hb, 2 * CHUNK, hd), BF16),
                        pltpu.VMEM((n_chunks, hb, CHUNK, CHUNK), BF16),
                        pltpu.VMEM((n_chunks, hb, hd, CHUNK), BF16),
                        pltpu.VMEM((n_chunks, hb, SUBLANES, hd), F32)],
        compiler_params=_params(3),
        name=name,
    )(qkv, qkv, kt, qkv, z, a_cols, b_cols, a_log.reshape(ng, 1, hb), dt_bias.reshape(ng, 1, hb),
      g_out.reshape(1, hd), state0)
    return o, s_out


def _band_attn_body(q_ref, k_ref, v_ref, bias_ref, o_ref, *, n_chunks, n_hist, scale):
    bias = bias_ref[...]
    key = lax.broadcasted_iota(jnp.int32, (CHUNK, BAND), 1)

    def chunk_step(c, carry):
        r0 = pl.multiple_of(c * CHUNK, CHUNK)
        q_c = q_ref[pl.ds(r0, CHUNK), :]
        k_b = k_ref[pl.ds(r0, BAND), :]
        v_b = v_ref[pl.ds(r0, BAND), :]
        sc = _dot_nt(q_c, k_b) * scale + bias
        sc = jnp.where(key + r0 >= WINDOW - n_hist, sc, NEG_INF)
        sc = sc - jnp.max(sc, axis=-1, keepdims=True)
        e = jnp.exp(sc)
        p = e / jnp.sum(e, axis=-1, keepdims=True)
        o_ref[pl.ds(r0, CHUNK), :] = _dot(p, v_b).astype(o_ref.dtype)
        return carry

    lax.fori_loop(0, n_chunks, chunk_step, 0, unroll=4 if n_chunks % 4 == 0 else 1)


def _band_attention(q, k_ext, v_ext, bias, *, n_hist, name):
    bsz, seq, hdim = q.shape
    heads = bias.shape[0]
    hd = hdim // heads
    body = functools.partial(_band_attn_body, n_chunks=seq // CHUNK, n_hist=n_hist, scale=hd ** -0.5)
    return pl.pallas_call(
        body,
        grid=(bsz, heads),
        in_specs=[pl.BlockSpec((None, seq, hd), lambda b, h: (b, 0, h)),
                  pl.BlockSpec((None, WINDOW + seq, hd), lambda b, h: (b, 0, h)),
                  pl.BlockSpec((None, WINDOW + seq, hd), lambda b, h: (b, 0, h)),
                  pl.BlockSpec((None, CHUNK, BAND), lambda b, h: (h, 0, 0))],
        out_specs=pl.BlockSpec((None, seq, hd), lambda b, h: (b, 0, h)),
        out_shape=jax.ShapeDtypeStruct((bsz, seq, hdim), BF16),
        compiler_params=_params(2),
        name=name,
    )(q, k_ext, v_ext, bias)


def _pad_cols(w, mult):
    pad = (-w.shape[1]) % mult
    return jnp.pad(w, ((0, 0), (0, pad))) if pad else w


def _run_trunk(x, p, conv_a, conv_b, ssm_b, hist_k, hist_v, conv_f, prm, tag):
    h = x
    bsz, seq, d = x.shape
    depth = prm["g_mix"].shape[0]
    keep = min(WINDOW, seq)
    out_a, out_bc, out_bs, out_k, out_v, out_f = [], [], [], [], [], []
    y = None
    for i in range(depth):
        kind, j = i % 3, i // 3
        if kind == 0:
            h, st = _conv_mlp(h, prm["g_mix"][i], prm["w_a_in"][j], prm["w_a_conv"][j], None, conv_a[j],
                              prm["w_a_out"][j], mode="a", name=f"{tag}_mixer_a{i}")
            out_a.append(st)
        elif kind == 1:
            vh, hd = ssm_b.shape[2], ssm_b.shape[3]
            vd = vh * hd
            conv_dim = prm["w_b_conv"].shape[-1]
            kd = (conv_dim - vd) // 2
            w_in = prm["w_b_in"][j]
            tn = _proj_tile(kd)
            epi = functools.partial(_delta_qkv_epilogue, n_q=kd // tn, n_k=kd // tn, hd=hd)
            qkv, st_c = _norm_proj(h, prm["g_mix"][i], w_in[:, :conv_dim], epilogue=epi, col_group=kd,
                                   w_conv=prm["w_b_conv"][j], hist=conv_b[j], name=f"{tag}_delta_qkv{i}")
            z = _norm_proj(h, prm["g_mix"][i], w_in[:, conv_dim:conv_dim + vd], epilogue=_store_epilogue,
                           name=f"{tag}_delta_z{i}")
            ba = _norm_proj(h, prm["g_mix"][i], _pad_cols(w_in[:, conv_dim + vd:], LANES),
                            epilogue=_store_epilogue, name=f"{tag}_delta_ba{i}")
            o, st_s = _delta_rule(qkv, z, ba, prm["b_a_log"][j], prm["b_dt_bias"][j], prm["g_b_out"][j],
                                  ssm_b[j], kd=kd, name=f"{tag}_delta_rule{i}")
            h = _out_proj(h, o, prm["w_b_out"][j], name=f"{tag}_delta_out{i}")
            out_bc.append(st_c)
            out_bs.append(st_s)
        else:
            heads, hd = hist_k.shape[-2], hist_k.shape[-1]
            hdim = heads * hd
            n_hist = hist_k.shape[2]
            tn = _proj_tile(hdim)
            epi = functools.partial(_attn_qkv_epilogue, n_q=hdim // tn, hd=hd)
            qkv = _norm_proj(h, prm["g_mix"][i], prm["w_c_qkv"][j], epilogue=epi, col_group=hdim,
                             extra=(prm["g_c_q"][j].reshape(1, hd), prm["g_c_k"][j].reshape(1, hd)),
                             name=f"{tag}_attn_qkv{i}")
            q, k_new, v_new = qkv[:, :, :hdim], qkv[:, :, hdim:2 * hdim], qkv[:, :, 2 * hdim:]

            def extend(new, hist):
                return jnp.concatenate([jnp.zeros((bsz, WINDOW - n_hist, hdim), F32),
                                        hist.reshape(bsz, n_hist, hdim), new], axis=1)

            rel_bias = prm["c_rel_bias"][j]
            clip = (rel_bias.shape[-1] - 1) // 2
            rel = WINDOW + jnp.arange(CHUNK)[:, None] - jnp.arange(BAND)[None, :]
            bias = rel_bias[:, jnp.clip(rel, -clip, clip) + clip]
            o = _band_attention(q, extend(k_new, hist_k[j]), extend(v_new, hist_v[j]), bias,
                                n_hist=n_hist, name=f"{tag}_attn{i}")
            h = _out_proj(h, o, prm["w_c_out"][j], name=f"{tag}_attn_out{i}")
            out_k.append(k_new[:, seq - keep:].reshape(bsz, keep, heads, hd))
            out_v.append(v_new[:, seq - keep:].reshape(bsz, keep, heads, hd))
        h, st = _conv_mlp(h, prm["g_ffn"][i], prm["w_f_up"][i], prm["w_f_conv"][i], prm["b_f_conv"][i],
                          conv_f[i], prm["w_f_down"][i], mode="ffn", name=f"{tag}_ffn{i}")
        out_f.append(st)
        h, y = _ple(h, p[i], prm["g_ple"][i], prm["w_ple_gate"][i], prm["w_ple_proj"][i],
                    prm["g_final"] if i == depth - 1 else None, name=f"{tag}_ple{i}")
    return (y, jnp.stack(out_a), jnp.stack(out_bc), jnp.stack(out_bs), jnp.stack(out_k), jnp.stack(out_v),
            jnp.stack(out_f))


def kernel(x_prompt, x_sample, state_a_conv, state_b_conv, state_b_ssm, cache_c_k, cache_c_v, state_ffn_conv,
           p_prompt, p_sample, g_mix, g_ffn, w_a_in, w_a_conv, w_a_out, w_b_in, w_b_conv, b_a_log, b_dt_bias,
           g_b_out, w_b_out, w_c_qkv, g_c_q, g_c_k, c_rel_bias, w_c_out, w_f_up, w_f_conv, b_f_conv, w_f_down,
           g_ple, w_ple_gate, w_ple_proj, g_final):
    prm = dict(g_mix=g_mix, g_ffn=g_ffn, w_a_in=w_a_in.astype(BF16), w_a_conv=w_a_conv,
               w_a_out=w_a_out.astype(BF16), w_b_in=w_b_in.astype(BF16), w_b_conv=w_b_conv, b_a_log=b_a_log,
               b_dt_bias=b_dt_bias, g_b_out=g_b_out, w_b_out=w_b_out.astype(BF16), w_c_qkv=w_c_qkv.astype(BF16),
               g_c_q=g_c_q, g_c_k=g_c_k, c_rel_bias=c_rel_bias, w_c_out=w_c_out.astype(BF16),
               w_f_up=w_f_up.astype(BF16), w_f_conv=w_f_conv, b_f_conv=b_f_conv, w_f_down=w_f_down.astype(BF16),
               g_ple=g_ple, w_ple_gate=w_ple_gate.astype(BF16), w_ple_proj=w_ple_proj.astype(BF16),
               g_final=g_final)
    bp = x_prompt.shape[0]
    dt = x_prompt.dtype
    zeros_like_state = lambda s: jnp.zeros((s.shape[0], bp) + s.shape[2:], dt)
    heads, hd = cache_c_k.shape[-2:]
    empty_kv = jnp.zeros((cache_c_k.shape[0], bp, 0, heads, hd), dt)
    prompt = _run_trunk(x_prompt, p_prompt, zeros_like_state(state_a_conv), zeros_like_state(state_b_conv),
                        zeros_like_state(state_b_ssm), empty_kv, empty_kv, zeros_like_state(state_ffn_conv),
                        prm, "prompt")
    sample = _run_trunk(x_sample, p_sample, state_a_conv, state_b_conv, state_b_ssm, cache_c_k, cache_c_v,
                        state_ffn_conv, prm, "sample")
    return (prompt[0], sample[0]) + prompt[1:] + sample[1:]
```

```python
import functools

import jax
import jax.numpy as jnp
from jax import lax
from jax.experimental import pallas as pl
from jax.experimental.pallas import tpu as pltpu

F32 = jnp.float32
BF16 = jnp.bfloat16

EPS = 1e-6
CHUNK = 64
LEFT_CHUNKS = 8
WINDOW = LEFT_CHUNKS * CHUNK
BAND = WINDOW + CHUNK
NEG_INF = -1e30

LANES = 128
SUBLANES = 8
HALO = SUBLANES
VMEM_LIMIT_BYTES = 56 * 1024 * 1024


def _params(n_axes):
    return pltpu.CompilerParams(dimension_semantics=("arbitrary",) * n_axes,
                                vmem_limit_bytes=VMEM_LIMIT_BYTES)


def _rms(x, g):
    return x * lax.rsqrt(jnp.mean(x * x, axis=-1, keepdims=True) + EPS) * g


def _silu(x):
    return x * jax.nn.sigmoid(x)


def _dot(a, b):
    return jnp.dot(a.astype(BF16), b.astype(BF16), preferred_element_type=F32)


def _dot_nt(a, b):
    return lax.dot_general(a.astype(BF16), b.astype(BF16), (((1,), (1,)), ((), ())),
                           preferred_element_type=F32)


def _largest_tile(n, target, quantum):
    if n <= target:
        return n
    t = (target // quantum) * quantum
    while t > quantum and n % t:
        t -= quantum
    assert n % t == 0, (n, target, quantum)
    return t


MXU_WIDTH = 256


def _proj_tile(n):
    return _largest_tile(n, 4 * MXU_WIDTH, LANES)


def _n_sub(tile):
    return tile // MXU_WIDTH if tile % MXU_WIDTH == 0 else 1


def _seq_tiles(bsz, seq, target=512):
    if seq >= target:
        return 1, _largest_tile(seq, target, CHUNK)
    bb = max(1, min(bsz, target // seq))
    while bsz % bb:
        bb -= 1
    return bb, seq


def _pad_hist(hist):
    b, r, c = hist.shape
    return jnp.concatenate([jnp.zeros((b, HALO - r, c), hist.dtype), hist], axis=1)


def _causal_conv(xx_ref, carry_ref, hist_ref, nh_ref, wc_ref, width, ts, at_seq_start):
    halo = jnp.where(at_seq_start, hist_ref[...], carry_ref[...])
    x = xx_ref[...]
    carry_ref[...] = x[:, ts - HALO:, :]
    nh_ref[...] = xx_ref[:, ts - (width - 1):, :]
    row = lax.broadcasted_iota(jnp.int32, (1, HALO, 1), 1)
    y = None
    for t in range(width):
        shift = width - 1 - t
        if shift:
            xr = pltpu.roll(x, shift, axis=1)
            head = jnp.where(row < shift, pltpu.roll(halo, shift, axis=1), xr[:, :HALO, :])
            xr = jnp.concatenate([head, xr[:, HALO:, :]], axis=1)
        else:
            xr = x
        term = xr * wc_ref[t:t + 1, :][None]
        y = term if y is None else y + term
    return y


def _conv_mlp_body(*refs, mode, width, bb, ts, tf, nf, d, n_sub):
    n_up = 2 if mode == "ffn" else 3
    n_cv = 2 if mode == "ffn" else 1
    it = iter(refs)
    h_ref, g_ref = next(it), next(it)
    wu = [next(it) for _ in range(n_up)]
    wc = [next(it) for _ in range(n_cv)]
    bc = [next(it) for _ in range(n_cv)] if mode == "ffn" else None
    hist = [next(it) for _ in range(n_cv)]
    wd_ref = next(it)
    out_ref = next(it)
    nh = [next(it) for _ in range(n_cv)]
    hn_s, acc_s, xx_s, carry_s = next(it), next(it), next(it), next(it)
    u_s = next(it) if mode != "ffn" else None

    s = pl.program_id(1)
    j = pl.program_id(2)
    m = bb * ts

    @pl.when(jnp.logical_and(jnp.logical_and(pl.program_id(0) == 0, s == 0), j == 0))
    def _():
        carry_s[...] = jnp.zeros_like(carry_s)

    @pl.when(j == 0)
    def _():
        x = h_ref[...].reshape(m, d)
        hn_s[...] = _rms(x, g_ref[...]).astype(BF16)
        acc_s[...] = x

    tsub = tf // n_sub
    hn = hn_s[...]
    for sub in range(n_sub):
        cols = pl.ds(sub * tsub, tsub)
        for g, w in enumerate(wu):
            r = jnp.dot(hn, w[:, cols], preferred_element_type=F32).reshape(bb, ts, tsub)
            if mode == "ffn":
                xx_s[g, :, :, cols] = r
            else:
                u_s[g, :, :, cols] = r
    for sub in range(n_sub):
        cols = pl.ds(sub * tsub, tsub)
        if mode != "ffn":
            xx_s[0, :, :, cols] = u_s[1, :, :, cols] * u_s[2, :, :, cols]
        ys = []
        for cg in range(n_cv):
            y = _causal_conv(xx_s.at[cg, :, :, cols], carry_s.at[j * n_cv + cg, :, :, cols],
                             hist[cg].at[:, :, cols], nh[cg].at[:, :, cols], wc[cg].at[:, cols],
                             width, ts, s == 0)
            if bc is not None:
                y = y + bc[cg][:, cols][None]
            ys.append(y)
        act = _silu(ys[0]) * ys[1] if mode == "ffn" else u_s[0, :, :, cols] * ys[0]
        acc_s[...] += jnp.dot(act.reshape(m, tsub).astype(BF16), wd_ref[cols, :], preferred_element_type=F32)

    @pl.when(j == nf - 1)
    def _():
        out_ref[...] = acc_s[...].reshape(bb, ts, d)


def _conv_mlp(h, gain, w_up, w_conv, b_conv, hist, w_down, *, mode, name):
    bsz, seq, d = h.shape
    f = w_down.shape[0]
    width = w_conv.shape[0]
    n_up = 2 if mode == "ffn" else 3
    n_cv = 2 if mode == "ffn" else 1
    bb, ts = _seq_tiles(bsz, seq)
    tf = _largest_tile(f, 512, LANES)
    nf = f // tf
    hist8 = _pad_hist(hist)

    in_specs = [pl.BlockSpec((bb, ts, d), lambda b, s, j: (b, s, 0)),
                pl.BlockSpec((1, d), lambda b, s, j: (0, 0))]
    args = [h, gain.reshape(1, d)]
    for g in range(n_up):
        in_specs.append(pl.BlockSpec((d, tf), lambda b, s, j, g=g: (0, g * nf + j)))
        args.append(w_up)
    for g in range(n_cv):
        in_specs.append(pl.BlockSpec((width, tf), lambda b, s, j, g=g: (0, g * nf + j)))
        args.append(w_conv)
    if mode == "ffn":
        for g in range(n_cv):
            in_specs.append(pl.BlockSpec((1, tf), lambda b, s, j, g=g: (0, g * nf + j)))
            args.append(b_conv.reshape(1, -1))
    for g in range(n_cv):
        in_specs.append(pl.BlockSpec((bb, HALO, tf), lambda b, s, j, g=g: (b, 0, g * nf + j)))
        args.append(hist8)
    in_specs.append(pl.BlockSpec((tf, d), lambda b, s, j: (j, 0)))
    args.append(w_down)

    out_shape = [jax.ShapeDtypeStruct((bsz, seq, d), F32)]
    out_specs = [pl.BlockSpec((bb, ts, d), lambda b, s, j: (b, s, 0))]
    for g in range(n_cv):
        out_shape.append(jax.ShapeDtypeStruct((bsz, seq // ts, width - 1, f), F32))
        out_specs.append(pl.BlockSpec((bb, None, width - 1, tf), lambda b, s, j: (b, s, 0, j)))

    n_sub = _n_sub(tf)
    body = functools.partial(_conv_mlp_body, mode=mode, width=width, bb=bb, ts=ts, tf=tf, nf=nf, d=d,
                             n_sub=n_sub)
    outs = pl.pallas_call(
        body,
        grid=(bsz // bb, seq // ts, nf),
        in_specs=in_specs,
        out_specs=out_specs,
        out_shape=out_shape,
        scratch_shapes=[pltpu.VMEM((bb * ts, d), BF16),
                        pltpu.VMEM((bb * ts, d), F32),
                        pltpu.VMEM((n_cv, bb, ts, tf), F32),
                        pltpu.VMEM((nf * n_cv, bb, HALO, tf), F32)]
        + ([pltpu.VMEM((n_up, bb, ts, tf), F32)] if mode != "ffn" else []),
        compiler_params=_params(3),
        name=name,
    )(*args)
    return outs[0], jnp.concatenate([o[:, -1] for o in outs[1:]], axis=-1)


def _proj_body(*refs, width, bb, ts, tn, d, n_extra, epilogue, n_sub):
    conv = width > 0
    it = iter(refs)
    h_ref, g_ref, w_ref = next(it), next(it), next(it)
    if conv:
        wc_ref, hist_ref = next(it), next(it)
    extra = [next(it) for _ in range(n_extra)]
    out_ref = next(it)
    if conv:
        nh_ref = next(it)
    hn_s = next(it)
    if conv:
        xx_s, carry_s = next(it), next(it)
    elif epilogue is not None:
        y_s = next(it)

    s = pl.program_id(1)
    j = pl.program_id(2)
    m = bb * ts

    if conv:
        @pl.when(jnp.logical_and(jnp.logical_and(pl.program_id(0) == 0, s == 0), j == 0))
        def _():
            carry_s[...] = jnp.zeros_like(carry_s)

    @pl.when(j == 0)
    def _():
        hn_s[...] = _rms(h_ref[...].reshape(m, d), g_ref[...]).astype(BF16)

    tsub = tn // n_sub
    hn = hn_s[...]
    raw = xx_s if conv else (out_ref if epilogue is None else y_s)
    for sub in range(n_sub):
        cols = pl.ds(sub * tsub, tsub)
        raw[:, :, cols] = jnp.dot(hn, w_ref[:, cols], preferred_element_type=F32).reshape(bb, ts, tsub)
    if epilogue is None:
        return
    for sub in range(n_sub):
        cols = pl.ds(sub * tsub, tsub)
        if conv:
            y = _causal_conv(xx_s.at[:, :, cols], carry_s.at[j, :, :, cols], hist_ref.at[:, :, cols],
                             nh_ref.at[:, :, cols], wc_ref.at[:, cols], width, ts, s == 0)
        else:
            y = y_s[:, :, cols]
        epilogue(j, y, extra, out_ref.at[:, :, cols])


def _norm_proj(h, gain, w, *, epilogue, extra=(), w_conv=None, hist=None, out_dtype=F32, col_group=None,
               name):
    bsz, seq, d = h.shape
    n = w.shape[1]
    conv = w_conv is not None
    width = w_conv.shape[0] if conv else 0
    bb, ts = _seq_tiles(bsz, seq)
    tn = _proj_tile(n if col_group is None else col_group)
    nt = n // tn

    in_specs = [pl.BlockSpec((bb, ts, d), lambda b, s, j: (b, s, 0)),
                pl.BlockSpec((1, d), lambda b, s, j: (0, 0)),
                pl.BlockSpec((d, tn), lambda b, s, j: (0, j))]
    args = [h, gain.reshape(1, d), w]
    if conv:
        in_specs += [pl.BlockSpec((width, tn), lambda b, s, j: (0, j)),
                     pl.BlockSpec((bb, HALO, tn), lambda b, s, j: (b, 0, j))]
        args += [w_conv, _pad_hist(hist)]
    for e in extra:
        in_specs.append(pl.BlockSpec(e.shape, lambda b, s, j, nd=e.ndim: (0,) * nd))
        args.append(e)

    out_shape = [jax.ShapeDtypeStruct((bsz, seq, n), out_dtype)]
    out_specs = [pl.BlockSpec((bb, ts, tn), lambda b, s, j: (b, s, j))]
    scratch = [pltpu.VMEM((bb * ts, d), BF16)]
    if conv:
        out_shape.append(jax.ShapeDtypeStruct((bsz, seq // ts, width - 1, n), F32))
        out_specs.append(pl.BlockSpec((bb, None, width - 1, tn), lambda b, s, j: (b, s, 0, j)))
        scratch += [pltpu.VMEM((bb, ts, tn), F32), pltpu.VMEM((nt, bb, HALO, tn), F32)]
    elif epilogue is not None:
        scratch.append(pltpu.VMEM((bb, ts, tn), F32))

    body = functools.partial(_proj_body, width=width, bb=bb, ts=ts, tn=tn, d=d, n_extra=len(extra),
                             epilogue=epilogue, n_sub=_n_sub(tn))
    outs = pl.pallas_call(
        body,
        grid=(bsz // bb, seq // ts, nt),
        in_specs=in_specs,
        out_specs=out_specs,
        out_shape=out_shape,
        scratch_shapes=scratch,
        compiler_params=_params(3),
        name=name,
    )(*args)
    return (outs[0], outs[1][:, -1]) if conv else outs[0]


def _delta_qkv_epilogue(j, y, extra, out_ref, *, n_q, n_k, hd):
    y = _silu(y)
    heads = y.shape[-1] // hd
    q_scale = jnp.where(j < n_q, hd ** -0.5, 1.0)
    is_qk = j < n_q + n_k
    for i in range(heads):
        yh = y[:, :, i * hd:(i + 1) * hd]
        inv = lax.rsqrt(jnp.sum(yh * yh, axis=-1, keepdims=True) + EPS) * q_scale
        out_ref[:, :, i * hd:(i + 1) * hd] = yh * jnp.where(is_qk, inv, 1.0)


def _attn_qkv_epilogue(j, y, extra, out_ref, *, n_q, hd):
    gq_ref, gk_ref = extra
    heads = y.shape[-1] // hd
    g = jnp.where(j < n_q, gq_ref[...], gk_ref[...])[None]
    is_qk = j < 2 * n_q
    for i in range(heads):
        yh = y[:, :, i * hd:(i + 1) * hd]
        out_ref[:, :, i * hd:(i + 1) * hd] = jnp.where(is_qk, _rms(yh, g), yh)


def _out_proj_body(h_ref, x_ref, w_ref, out_ref):
    out_ref[...] = h_ref[...] + jnp.dot(x_ref[...], w_ref[...], preferred_element_type=F32)


def _out_proj(h, x, w, *, name):
    bsz, seq, d = h.shape
    k = x.shape[-1]
    t = bsz * seq
    tm = _largest_tile(t, 512, SUBLANES)
    tn = _largest_tile(d, 512, LANES)
    out = pl.pallas_call(
        _out_proj_body,
        grid=(t // tm, d // tn),
        in_specs=[pl.BlockSpec((tm, tn), lambda i, j: (i, j)),
                  pl.BlockSpec((tm, k), lambda i, j: (i, 0)),
                  pl.BlockSpec((k, tn), lambda i, j: (0, j))],
        out_specs=pl.BlockSpec((tm, tn), lambda i, j: (i, j)),
        out_shape=jax.ShapeDtypeStruct((t, d), F32),
        compiler_params=_params(2),
        name=name,
    )(h.reshape(t, d), x.reshape(t, k), w)
    return out.reshape(bsz, seq, d)


def _ple_body(*refs, final):
    if final:
        h_ref, p_ref, g_ref, wg_ref, wp_ref, gf_ref, out_ref, y_ref = refs
    else:
        h_ref, p_ref, g_ref, wg_ref, wp_ref, out_ref = refs
    x = h_ref[...]
    gate = jax.nn.sigmoid(_dot(_rms(x, g_ref[...]), wg_ref[...]))
    hn = x + _dot(p_ref[...], wp_ref[...]) * gate
    out_ref[...] = hn
    if final:
        y_ref[...] = _rms(hn, gf_ref[...])


def _ple(h, p_all, layer, gain, w_gate, w_proj, g_final=None, *, name):
    bsz, seq, d = h.shape
    pd = p_all.shape[-1]
    t = bsz * seq
    tm = _largest_tile(t, 256, SUBLANES)
    final = g_final is not None
    in_specs = [pl.BlockSpec((tm, d), lambda i: (i, 0)),
                pl.BlockSpec((None, tm, pd), lambda i: (layer, i, 0)),
                pl.BlockSpec((1, d), lambda i: (0, 0)),
                pl.BlockSpec((d, d), lambda i: (0, 0)),
                pl.BlockSpec((pd, d), lambda i: (0, 0))]
    args = [h.reshape(t, d), p_all.reshape(p_all.shape[0], t, pd), gain.reshape(1, d), w_gate, w_proj]
    out_shape = [jax.ShapeDtypeStruct((t, d), F32)]
    out_specs = [pl.BlockSpec((tm, d), lambda i: (i, 0))]
    if final:
        in_specs.append(pl.BlockSpec((1, d), lambda i: (0, 0)))
        args.append(g_final.reshape(1, d))
        out_shape.append(jax.ShapeDtypeStruct((t, d), F32))
        out_specs.append(pl.BlockSpec((tm, d), lambda i: (i, 0)))
    outs = pl.pallas_call(
        functools.partial(_ple_body, final=final),
        grid=(t // tm,),
        in_specs=in_specs,
        out_specs=out_specs,
        out_shape=out_shape,
        compiler_params=_params(1),
        name=name,
    )(*args)
    if final:
        return outs[0].reshape(bsz, seq, d), outs[1].reshape(bsz, seq, d)
    return outs[0].reshape(bsz, seq, d), None


_INV_BASE_LOG2 = 3


def _bdot(a, b):
    return lax.dot_general(a.astype(BF16), b.astype(BF16), (((2,), (1,)), ((0,), (0,))),
                           preferred_element_type=F32)


def _bdot_nt(a, b):
    return lax.dot_general(a.astype(BF16), b.astype(BF16), (((2,), (2,)), ((0,), (0,))),
                           preferred_element_type=F32)


def _unit_lower_inverse(mneg, eye, same_blk):
    c = mneg.shape[1]
    m = jnp.where(same_blk[0], mneg, 0.0)
    p = eye + m
    m2 = _bdot(m, m)
    r = _bdot(jnp.concatenate([m2, p], axis=1), m2)
    p = p + r[:, c:]
    x = p + _bdot(r[:, :c], p)
    for lvl in range(1, len(same_blk)):
        off = jnp.where(jnp.logical_and(same_blk[lvl], jnp.logical_not(same_blk[lvl - 1])), mneg, 0.0)
        x = x + _bdot(_bdot(x, off), x)
    return x


def _delta_body(q_ref, k_ref, kt_ref, v_ref, z_ref, a_ref, b_ref, alog_ref, dtb_ref, gout_ref, s0_ref,
                o_ref, sout_ref, state_s, u_s, wq_s, a_s, kd_s, dl_s, *, hb, rep, hd, n_chunks, pb):
    c_blk = pl.program_id(2)
    C = CHUNK
    qb = hb // rep
    G = pb * hb

    @pl.when(c_blk == 0)
    def _():
        state_s[...] = s0_ref[...]

    row = lax.broadcasted_iota(jnp.int32, (C, C), 0)
    col = lax.broadcasted_iota(jnp.int32, (C, C), 1)
    tril = col <= row
    triu = row <= col
    strict = col < row
    eye = (col == row).astype(F32)
    same_blk = [jnp.right_shift(row, s) == jnp.right_shift(col, s) for s in range(_INV_BASE_LOG2, 7)]
    neg_rate = -jnp.exp(alog_ref[...])
    dt_bias = dtb_ref[...]
    g_out = gout_ref[...]

    def per_value_head(x):
        return jnp.stack([x[(i // hb) * qb + (i % hb) // rep] for i in range(G)])

    def phase1(step, carry):
        base = step * pb
        ks, qs, kts, g_cols, betas, vs = [], [], [], [], [], []
        for p in range(pb):
            r0 = pl.multiple_of((base + p) * C, C)
            a_c = a_ref[pl.ds(r0, C), :] + dt_bias
            g_all = neg_rate * (jnp.maximum(a_c, 0.0) + jnp.log1p(jnp.exp(-jnp.abs(a_c))))
            beta_all = jax.nn.sigmoid(b_ref[pl.ds(r0, C), :])
            for iq in range(qb):
                ks.append(k_ref[pl.ds(r0, C), iq * hd:(iq + 1) * hd])
                qs.append(q_ref[pl.ds(r0, C), iq * hd:(iq + 1) * hd])
                kts.append(kt_ref[iq, base + p])
            for iv in range(hb):
                g_cols.append(g_all[:, iv:iv + 1])
                betas.append(beta_all[:, iv:iv + 1])
                vs.append(v_ref[pl.ds(r0, C), iv * hd:(iv + 1) * hd])
        k_st, q_st = jnp.stack(ks), jnp.stack(qs)
        kq = _bdot_nt(jnp.concatenate([k_st, q_st], axis=1), k_st)
        kk, qk = per_value_head(kq[:, :C]), per_value_head(kq[:, C:])
        k_g, q_g, kt_g = per_value_head(k_st), per_value_head(q_st), per_value_head(jnp.stack(kts))
        g_col, beta, v_g = jnp.stack(g_cols), jnp.stack(betas), jnp.stack(vs)

        gc_row = jnp.sum(jnp.where(triu, g_col, 0.0), axis=1, keepdims=True)
        gc_col = jnp.sum(eye * gc_row, axis=2, keepdims=True)
        decay = jnp.where(tril, jnp.exp(jnp.where(tril, gc_col - gc_row, 0.0)), 0.0)
        mneg = jnp.where(strict, -(beta * kk) * decay, 0.0)
        a_intra = jnp.where(tril, qk * decay, 0.0)
        tmat = _unit_lower_inverse(mneg, eye, same_blk)
        e_col = jnp.exp(gc_col)
        uw = _bdot(tmat, jnp.concatenate([v_g * beta, k_g * (beta * e_col)], axis=2))
        g_last = gc_row[:, :, C - 1:C]
        kd = kt_g * jnp.exp(g_last - gc_row)
        wq = jnp.concatenate([uw[:, :, hd:], q_g * e_col], axis=1)

        u_s[pl.ds(base, pb)] = uw[:, :, :hd].reshape(pb, hb, C, hd)
        wq_s[pl.ds(base, pb)] = wq.astype(BF16).reshape(pb, hb, 2 * C, hd)
        a_s[pl.ds(base, pb)] = a_intra.astype(BF16).reshape(pb, hb, C, C)
        kd_s[pl.ds(base, pb)] = kd.astype(BF16).reshape(pb, hb, hd, C)
        dl_s[pl.ds(base, pb)] = jnp.broadcast_to(jnp.exp(g_last), (G, SUBLANES, hd)).reshape(pb, hb, SUBLANES, hd)
        return carry

    lax.fori_loop(0, n_chunks // pb, phase1, 0)

    def phase2(ci, carry):
        r0 = pl.multiple_of(ci * C, C)
        state = state_s[...]
        ws_qs = _bdot(wq_s[ci], state)
        v_new = u_s[ci] - ws_qs[:, :C]
        o = ws_qs[:, C:] + _bdot(a_s[ci], v_new)
        state_s[...] = state * dl_s[ci][:, 0:1, :] + _bdot(kd_s[ci], v_new)
        o = _rms(o, g_out)
        for iv in range(hb):
            z_c = z_ref[pl.ds(r0, C), iv * hd:(iv + 1) * hd]
            o_ref[pl.ds(r0, C), iv * hd:(iv + 1) * hd] = (o[iv] * _silu(z_c)).astype(o_ref.dtype)
        return carry

    lax.fori_loop(0, n_chunks, phase2, 0)

    @pl.when(c_blk == pl.num_programs(2) - 1)
    def _():
        sout_ref[...] = state_s[...]


def _delta_rule(qkv, z, ba, a_log, dt_bias, g_out, state0, *, kd, name):
    bsz, seq, _ = qkv.shape
    _, vh, hd, _ = state0.shape
    vd = vh * hd
    qh = kd // hd
    rep = vh // qh
    hb = min(vh, 4 * rep)
    qb = hb // rep
    ng = vh // hb
    cs = _largest_tile(seq, 4 * CHUNK, CHUNK)
    n_chunks = cs // CHUNK
    nc = seq // CHUNK
    pb = 2 if n_chunks % 2 == 0 else 1

    k = qkv[:, :, kd:2 * kd]
    kt = k.reshape(bsz, nc, CHUNK, qh, hd).transpose(0, 3, 1, 4, 2)
    b_cols = ba[:, :, :vh].reshape(bsz, seq, ng, hb).transpose(0, 2, 1, 3)
    a_cols = ba[:, :, vh:2 * vh].reshape(bsz, seq, ng, hb).transpose(0, 2, 1, 3)

    body = functools.partial(_delta_body, hb=hb, rep=rep, hd=hd, n_chunks=n_chunks, pb=pb)
    o, s_out = pl.pallas_call(
        body,
        grid=(bsz, ng, seq // cs),
        in_specs=[
            pl.BlockSpec((None, cs, qb * hd), lambda b, g, c: (b, c, g)),
            pl.BlockSpec((None, cs, qb * hd), lambda b, g, c: (b, c, kd // (qb * hd) + g)),
            pl.BlockSpec((None, qb, n_chunks, hd, CHUNK), lambda b, g, c: (b, g, c, 0, 0)),
            pl.BlockSpec((None, cs, hb * hd), lambda b, g, c: (b, c, 2 * kd // (hb * hd) + g)),
            pl.BlockSpec((None, cs, hb * hd), lambda b, g, c: (b, c, g)),
            pl.BlockSpec((None, None, cs, hb), lambda b, g, c: (b, g, c, 0)),
            pl.BlockSpec((None, None, cs, hb), lambda b, g, c: (b, g, c, 0)),
            pl.BlockSpec((None, 1, hb), lambda b, g, c: (g, 0, 0)),
            pl.BlockSpec((None, 1, hb), lambda b, g, c: (g, 0, 0)),
            pl.BlockSpec((1, hd), lambda b, g, c: (0, 0)),
            pl.BlockSpec((None, hb, hd, hd), lambda b, g, c: (b, g, 0, 0)),
        ],
        out_specs=[pl.BlockSpec((None, cs, hb * hd), lambda b, g, c: (b, c, g)),
                   pl.BlockSpec((None, hb, hd, hd), lambda b, g, c: (b, g, 0, 0))],
        out_shape=[jax.ShapeDtypeStruct((bsz, seq, vd), BF16),
                   jax.ShapeDtypeStruct((bsz, vh, hd, hd), F32)],
        scratch_shapes=[pltpu.VMEM((hb, hd, hd), F32),
                        pltpu.VMEM((n_chunks, hb, CHUNK, hd), F32),
                        pltpu.VMEM((n_chunks, hb, 2 * CHUNK, hd), BF16),
                        pltpu.VMEM((n_chunks, hb, CHUNK, CHUNK), BF16),
                        pltpu.VMEM((n_chunks, hb, hd, CHUNK), BF16),
                        pltpu.VMEM((n_chunks, hb, SUBLANES, hd), F32)],
        compiler_params=_params(3),
        name=name,
    )(qkv, qkv, kt, qkv, z, a_cols, b_cols, a_log.reshape(ng, 1, hb), dt_bias.reshape(ng, 1, hb),
      g_out.reshape(1, hd), state0)
    return o, s_out


def _band_attn_body(*refs, n_groups, grp, n_hist, scale):
    if n_hist:
        q_ref, k_ref, v_ref, hk_ref, hv_ref, bias_ref, o_ref, kx_s, vx_s = refs
    else:
        q_ref, k_ref, v_ref, bias_ref, o_ref, kx_s, vx_s = refs
    hd = q_ref.shape[-1]
    rows_q, rows_k = grp * CHUNK, (grp + LEFT_CHUNKS) * CHUNK

    for new_ref, hist_ref, x_s in ((k_ref, hk_ref if n_hist else None, kx_s),
                                   (v_ref, hv_ref if n_hist else None, vx_s)):
        if n_hist < WINDOW:
            x_s[0:WINDOW - n_hist, :] = jnp.zeros((WINDOW - n_hist, hd), BF16)
        if n_hist:
            x_s[WINDOW - n_hist:WINDOW, :] = hist_ref[...].astype(BF16)
        x_s[WINDOW:, :] = new_ref[...].astype(BF16)

    bias = bias_ref[...]
    key = lax.broadcasted_iota(jnp.int32, (rows_q, rows_k), 1)

    def group_step(g, carry):
        r0 = pl.multiple_of(g * rows_q, rows_q)
        sc = _dot_nt(q_ref[pl.ds(r0, rows_q), :], kx_s[pl.ds(r0, rows_k), :]) * scale + bias
        if n_hist < WINDOW:
            sc = jnp.where(key + r0 >= WINDOW - n_hist, sc, NEG_INF)
        e = jnp.exp(sc - jnp.max(sc, axis=-1, keepdims=True))
        o = _dot(e, vx_s[pl.ds(r0, rows_k), :]) / jnp.sum(e, axis=-1, keepdims=True)
        o_ref[pl.ds(r0, rows_q), :] = o.astype(o_ref.dtype)
        return carry

    lax.fori_loop(0, n_groups, group_step, 0)


def _band_bias(rel_bias, grp):
    heads, n_rel = rel_bias.shape
    clip = (n_rel - 1) // 2
    d = jnp.arange(BAND + CHUNK - 1) - (BAND - 1)
    vec = rel_bias[:, jnp.clip(WINDOW + d, -clip, clip) + clip]
    rev = vec[:, ::-1]
    band = jnp.stack([rev[:, CHUNK - 1 - r:CHUNK - 1 - r + BAND] for r in range(CHUNK)], axis=1)
    return jnp.concatenate(
        [jnp.pad(band, ((0, 0), (0, 0), (i * CHUNK, (grp - 1 - i) * CHUNK)), constant_values=NEG_INF)
         for i in range(grp)], axis=1)


def _band_attention(qkv, hist_k, hist_v, rel_bias, *, name):
    bsz, seq, _ = qkv.shape
    heads = rel_bias.shape[0]
    n_hist, hdim = hist_k.shape[1], hist_k.shape[2]
    hd = hdim // heads
    n_chunks = seq // CHUNK
    grp = 4 if n_chunks % 4 == 0 else 1
    rows_q, rows_k = grp * CHUNK, (grp + LEFT_CHUNKS) * CHUNK
    in_specs = [pl.BlockSpec((None, seq, hd), lambda b, h: (b, 0, h)),
                pl.BlockSpec((None, seq, hd), lambda b, h: (b, 0, heads + h)),
                pl.BlockSpec((None, seq, hd), lambda b, h: (b, 0, 2 * heads + h))]
    args = [qkv, qkv, qkv]
    if n_hist:
        in_specs += [pl.BlockSpec((None, n_hist, hd), lambda b, h: (b, 0, h))] * 2
        args += [hist_k, hist_v]
    in_specs.append(pl.BlockSpec((None, rows_q, rows_k), lambda b, h: (h, 0, 0)))
    args.append(_band_bias(rel_bias, grp))
    body = functools.partial(_band_attn_body, n_groups=n_chunks // grp, grp=grp, n_hist=n_hist,
                             scale=hd ** -0.5)
    return pl.pallas_call(
        body,
        grid=(bsz, heads),
        in_specs=in_specs,
        out_specs=pl.BlockSpec((None, seq, hd), lambda b, h: (b, 0, h)),
        out_shape=jax.ShapeDtypeStruct((bsz, seq, hdim), BF16),
        scratch_shapes=[pltpu.VMEM((WINDOW + seq, hd), BF16), pltpu.VMEM((WINDOW + seq, hd), BF16)],
        compiler_params=_params(2),
        name=name,
    )(*args)


def _pad_cols(w, mult):
    pad = (-w.shape[1]) % mult
    return jnp.pad(w, ((0, 0), (0, pad))) if pad else w


def _run_trunk(x, p, conv_a, conv_b, ssm_b, hist_k, hist_v, conv_f, prm, tag):
    h = x
    bsz, seq, d = x.shape
    depth = prm["g_mix"].shape[0]
    keep = min(WINDOW, seq)
    out_a, out_bc, out_bs, out_k, out_v, out_f = [], [], [], [], [], []
    y = None
    for i in range(depth):
        kind, j = i % 3, i // 3
        if kind == 0:
            h, st = _conv_mlp(h, prm["g_mix"][i], prm["w_a_in"][j], prm["w_a_conv"][j], None, conv_a[j],
                              prm["w_a_out"][j], mode="a", name=f"{tag}_mixer_a{i}")
            out_a.append(st)
        elif kind == 1:
            hd = ssm_b.shape[3]
            w_qkv, w_z, w_ba = prm["w_b_in"][j]
            kd = (w_qkv.shape[1] - w_z.shape[1]) // 2
            tn = _proj_tile(kd)
            epi = functools.partial(_delta_qkv_epilogue, n_q=kd // tn, n_k=kd // tn, hd=hd)
            qkv, st_c = _norm_proj(h, prm["g_mix"][i], w_qkv, epilogue=epi, col_group=kd,
                                   w_conv=prm["w_b_conv"][j], hist=conv_b[j], name=f"{tag}_delta_qkv{i}")
            z = _norm_proj(h, prm["g_mix"][i], w_z, epilogue=None, name=f"{tag}_delta_z{i}")
            ba = _norm_proj(h, prm["g_mix"][i], w_ba, epilogue=None, name=f"{tag}_delta_ba{i}")
            o, st_s = _delta_rule(qkv, z, ba, prm["b_a_log"][j], prm["b_dt_bias"][j], prm["g_b_out"][j],
                                  ssm_b[j], kd=kd, name=f"{tag}_delta_rule{i}")
            h = _out_proj(h, o, prm["w_b_out"][j], name=f"{tag}_delta_out{i}")
            out_bc.append(st_c)
            out_bs.append(st_s)
        else:
            heads, hd = hist_k.shape[-2], hist_k.shape[-1]
            hdim = heads * hd
            n_hist = hist_k.shape[2]
            tn = _proj_tile(hdim)
            epi = functools.partial(_attn_qkv_epilogue, n_q=hdim // tn, hd=hd)
            qkv = _norm_proj(h, prm["g_mix"][i], prm["w_c_qkv"][j], epilogue=epi, col_group=hdim,
                             extra=(prm["g_c_q"][j].reshape(1, hd), prm["g_c_k"][j].reshape(1, hd)),
                             name=f"{tag}_attn_qkv{i}")
            o = _band_attention(qkv, hist_k[j].reshape(bsz, n_hist, hdim), hist_v[j].reshape(bsz, n_hist, hdim),
                                prm["c_rel_bias"][j], name=f"{tag}_attn{i}")
            h = _out_proj(h, o, prm["w_c_out"][j], name=f"{tag}_attn_out{i}")
            out_k.append(qkv[:, seq - keep:, hdim:2 * hdim].reshape(bsz, keep, heads, hd))
            out_v.append(qkv[:, seq - keep:, 2 * hdim:].reshape(bsz, keep, heads, hd))
        h, st = _conv_mlp(h, prm["g_ffn"][i], prm["w_f_up"][i], prm["w_f_conv"][i], prm["b_f_conv"][i],
                          conv_f[i], prm["w_f_down"][i], mode="ffn", name=f"{tag}_ffn{i}")
        out_f.append(st)
        h, y = _ple(h, p, i, prm["g_ple"][i], prm["w_ple_gate"][i], prm["w_ple_proj"][i],
                    prm["g_final"] if i == depth - 1 else None, name=f"{tag}_ple{i}")
    return (y, jnp.stack(out_a), jnp.stack(out_bc), jnp.stack(out_bs), jnp.stack(out_k), jnp.stack(out_v),
            jnp.stack(out_f))


def kernel(x_prompt, x_sample, state_a_conv, state_b_conv, state_b_ssm, cache_c_k, cache_c_v, state_ffn_conv,
           p_prompt, p_sample, g_mix, g_ffn, w_a_in, w_a_conv, w_a_out, w_b_in, w_b_conv, b_a_log, b_dt_bias,
           g_b_out, w_b_out, w_c_qkv, g_c_q, g_c_k, c_rel_bias, w_c_out, w_f_up, w_f_conv, b_f_conv, w_f_down,
           g_ple, w_ple_gate, w_ple_proj, g_final):
    def per_layer(w):
        return [w[i].astype(BF16) for i in range(w.shape[0])]

    conv_dim = w_b_conv.shape[-1]
    vd = state_b_ssm.shape[2] * state_b_ssm.shape[3]
    w_b_split = [(w_b_in[i, :, :conv_dim].astype(BF16), w_b_in[i, :, conv_dim:conv_dim + vd].astype(BF16),
                  _pad_cols(w_b_in[i, :, conv_dim + vd:], LANES).astype(BF16)) for i in range(w_b_in.shape[0])]
    prm = dict(g_mix=g_mix, g_ffn=g_ffn, w_a_in=per_layer(w_a_in), w_a_conv=w_a_conv,
               w_a_out=per_layer(w_a_out), w_b_in=w_b_split, w_b_conv=w_b_conv, b_a_log=b_a_log,
               b_dt_bias=b_dt_bias, g_b_out=g_b_out, w_b_out=per_layer(w_b_out), w_c_qkv=per_layer(w_c_qkv),
               g_c_q=g_c_q, g_c_k=g_c_k, c_rel_bias=c_rel_bias, w_c_out=per_layer(w_c_out),
               w_f_up=per_layer(w_f_up), w_f_conv=w_f_conv, b_f_conv=b_f_conv, w_f_down=per_layer(w_f_down),
               g_ple=g_ple, w_ple_gate=per_layer(w_ple_gate), w_ple_proj=per_layer(w_ple_proj),
               g_final=g_final)
    bp = x_prompt.shape[0]
    dt = x_prompt.dtype
    zeros_like_state = lambda s: jnp.zeros((s.shape[0], bp) + s.shape[2:], dt)
    heads, hd = cache_c_k.shape[-2:]
    empty_kv = jnp.zeros((cache_c_k.shape[0], bp, 0, heads, hd), dt)
    prompt = _run_trunk(x_prompt, p_prompt, zeros_like_state(state_a_conv), zeros_like_state(state_b_conv),
                        zeros_like_state(state_b_ssm), empty_kv, empty_kv, zeros_like_state(state_ffn_conv),
                        prm, "prompt")
    sample = _run_trunk(x_sample, p_sample, state_a_conv, state_b_conv, state_b_ssm, cache_c_k, cache_c_v,
                        state_ffn_conv, prm, "sample")
    return (prompt[0], sample[0]) + prompt[1:] + sample[1:]
```
